```python
import math
import jax
import jax.numpy as jnp
from jax import lax
import numpy as np

D_MODEL = 1024
BATCH = 8
SEQ = 4096
DEPTH = 2

HEAD_DIM = 64
N_HEADS = D_MODEL // HEAD_DIM
MIX_WIDTH = N_HEADS * HEAD_DIM
A_HEADS = N_HEADS // 2
B_HEADS = N_HEADS // 4
B_KV_HEADS = B_HEADS // 2
C_HEADS = N_HEADS // 4
C_KV_HEADS = C_HEADS // 2
DILATED_PATTERNS = ((128, 1), (512, 4), (2048, 16))
C_RADIUS = 128
Q_BLOCK = 128
GRID_W = 64
ROPE_THETA = 10000.0
D_FF = 2816
EPS = 1e-6
MASK_VALUE = -1e30
PROJ_SIZES = (A_HEADS * HEAD_DIM, A_HEADS * HEAD_DIM, A_HEADS * HEAD_DIM,
              B_HEADS * HEAD_DIM, B_KV_HEADS * HEAD_DIM, B_KV_HEADS * HEAD_DIM,
              C_HEADS * HEAD_DIM, C_KV_HEADS * HEAD_DIM, C_KV_HEADS * HEAD_DIM)
PROJ_WIDTH = sum(PROJ_SIZES)

kernel_name = 'hybrid_dilated_axial_window_encoder'


def rms_norm(x, g):
    xf = x.astype(jnp.float32)
    y = xf * lax.rsqrt(jnp.mean(xf * xf, axis=-1, keepdims=True) + EPS)
    return (y * g.astype(jnp.float32)).astype(x.dtype)


def rope_tables(pos, dim):
    inv_freq = 1.0 / (ROPE_THETA ** (jnp.arange(0, dim, 2, dtype=jnp.float32) / dim))
    ang = pos.astype(jnp.float32)[:, None] * inv_freq[None, :]
    ang = jnp.concatenate([ang, ang], axis=-1)
    return jnp.cos(ang), jnp.sin(ang)


def apply_rope(x, cos, sin):
    xf = x.astype(jnp.float32)
    half = xf.shape[-1] // 2
    rot = jnp.concatenate([-xf[..., half:], xf[..., :half]], axis=-1)
    return (xf * cos[:, None, :] + rot * sin[:, None, :]).astype(x.dtype)


def apply_axial_rope(x, cos_r, sin_r, cos_c, sin_c):
    half = x.shape[-1] // 2
    return jnp.concatenate([apply_rope(x[..., :half], cos_r, sin_r),
                            apply_rope(x[..., half:], cos_c, sin_c)], axis=-1)


def to_residue(x, d):
    b, s = x.shape[:2]
    rest = x.shape[2:]
    x = x.reshape((b, s // d, d) + rest)
    return jnp.moveaxis(x, 2, 1).reshape((b * d, s // d) + rest)


def from_residue(x, batch, d):
    n_slots = x.shape[1]
    rest = x.shape[2:]
    x = x.reshape((batch, d, n_slots) + rest)
    return jnp.moveaxis(x, 1, 2).reshape((batch, d * n_slots) + rest)


def banded_attention(q, k, v, radius, sink=None):
    n, seq_len, hq, dh = q.shape
    hkv = k.shape[2]
    rep = hq // hkv
    qb = math.gcd(radius, seq_len)
    nb = seq_len // qb
    kw = qb + 2 * radius
    pad = ((0, 0), (radius, radius), (0, 0), (0, 0))
    kp = jnp.pad(k, pad)
    vp = jnp.pad(v, pad)
    idx = (jnp.arange(nb) * qb)[:, None] + jnp.arange(kw)[None, :]
    kb = kp[:, idx]
    vb = vp[:, idx]
    qr = q.reshape(n, nb, qb, hkv, rep, dh)
    s = jnp.einsum('nbqgrd,nbkgd->nbgrqk', qr, kb,
                   preferred_element_type=jnp.float32) / math.sqrt(dh)
    kpos = idx - radius
    rel = jnp.arange(kw)[None, :] - radius - jnp.arange(qb)[:, None]
    valid = (jnp.abs(rel) <= radius)[None] & ((kpos >= 0) & (kpos < seq_len))[:, None, :]
    s = jnp.where(valid[None, :, None, None], s, MASK_VALUE)
    m = jnp.max(s, axis=-1)
    if sink is not None:
        sink_b = sink.astype(jnp.float32).reshape(hkv, rep)[None, None, :, :, None]
        m = jnp.maximum(m, sink_b)
    p = jnp.exp(s - m[..., None])
    denom = jnp.sum(p, axis=-1)
    if sink is not None:
        denom = denom + jnp.exp(sink_b - m)
    o = jnp.einsum('nbgrqk,nbkgd->nbqgrd', p.astype(v.dtype), vb,
                   preferred_element_type=jnp.float32)
    o = o / jnp.moveaxis(denom, -1, 2)[..., None]
    lse = jnp.moveaxis(m + jnp.log(denom), -1, 2).reshape(n, seq_len, hq)
    return o.reshape(n, seq_len, hq, dh).astype(v.dtype), lse


def dense_gqa_blocked(q, k, v):
    b, s, hq, dh = q.shape
    hkv = k.shape[2]
    rep = hq // hkv
    nb = s // Q_BLOCK
    qs = jnp.moveaxis(q.reshape(b, nb, Q_BLOCK, hkv, rep, dh), 1, 0)
    scale = 1.0 / math.sqrt(dh)

    def one_block(qblk):
        sc = jnp.einsum('bqgrd,bkgd->bgrqk', qblk, k,
                        preferred_element_type=jnp.float32) * scale
        p = jax.nn.softmax(sc, axis=-1)
        return jnp.einsum('bgrqk,bkgd->bqgrd', p.astype(v.dtype), v)

    o = lax.map(one_block, qs)
    return jnp.moveaxis(o, 0, 1).reshape(b, s, hq, dh)


def half_swiglu(x, g, w_gate, w_up, w_down):
    h = rms_norm(x, g)
    u = jax.nn.silu(h @ w_gate) * (h @ w_up)
    return x + 0.5 * (u @ w_down)


def hybrid_mixer(x, g_mix, w_in, a_qn, a_kn, b_qn, b_kn, c_qn, c_kn, c_sink, g_group, w_out,
                 cos1, sin1, cos_r, sin_r, cos_c, sin_c):
    b, s, _ = x.shape
    h = rms_norm(x, g_mix)
    proj = h @ w_in
    offsets = []
    acc = 0
    for size in PROJ_SIZES[:-1]:
        acc += size
        offsets.append(acc)
    aq, ak, av, bq, bk, bv, cq, ck, cv = jnp.split(proj, offsets, axis=-1)

    def heads(t, n_h):
        return t.reshape(b, s, n_h, HEAD_DIM)

    aq = apply_rope(rms_norm(heads(aq, A_HEADS), a_qn), cos1, sin1)
    ak = apply_rope(rms_norm(heads(ak, A_HEADS), a_kn), cos1, sin1)
    av = heads(av, A_HEADS)
    outs, lses = [], []
    for window, dil in DILATED_PATTERNS:
        o, lse = banded_attention(to_residue(aq, dil), to_residue(ak, dil), to_residue(av, dil),
                                  window // (2 * dil))
        outs.append(from_residue(o, b, dil))
        lses.append(from_residue(lse, b, dil))
    mix_w = jax.nn.softmax(jnp.stack(lses, axis=0), axis=0)
    oa = jnp.sum(mix_w[..., None] * jnp.stack(outs, axis=0).astype(jnp.float32),
                 axis=0).astype(x.dtype)

    bq = apply_axial_rope(rms_norm(heads(bq, B_HEADS), b_qn), cos_r, sin_r, cos_c, sin_c)
    bk = apply_axial_rope(rms_norm(heads(bk, B_KV_HEADS), b_kn), cos_r, sin_r, cos_c, sin_c)
    ob = dense_gqa_blocked(bq, bk, heads(bv, B_KV_HEADS))

    cq = apply_rope(rms_norm(heads(cq, C_HEADS), c_qn), cos1, sin1)
    ck = apply_rope(rms_norm(heads(ck, C_KV_HEADS), c_kn), cos1, sin1)
    oc, _ = banded_attention(cq, ck, heads(cv, C_KV_HEADS), C_RADIUS, c_sink)

    wa = A_HEADS * HEAD_DIM
    wb = B_HEADS * HEAD_DIM
    oa = rms_norm(oa.reshape(b, s, wa), g_group[:wa])
    ob = rms_norm(ob.reshape(b, s, wb), g_group[wa:wa + wb])
    oc = rms_norm(oc.reshape(b, s, -1), g_group[wa + wb:])
    return jnp.concatenate([oa, ob, oc], axis=-1) @ w_out


def setup_inputs(seed: int = 0) -> dict:
    key = jax.random.key(seed)
    ks = jax.random.split(key, 20)

    def w(k, shape, fan_in):
        return jax.random.normal(k, shape, jnp.float32) * fan_in ** -0.5

    def gain(k, shape):
        return 1.0 + 0.02 * jax.random.normal(k, shape, jnp.float32)

    return {
        'x': jax.random.normal(ks[0], (BATCH, SEQ, D_MODEL), jnp.float32),
        'ffn1_norm': gain(ks[1], (DEPTH, D_MODEL)),
        'ffn1_w_gate': w(ks[2], (DEPTH, D_MODEL, D_FF), D_MODEL),
        'ffn1_w_up': w(ks[3], (DEPTH, D_MODEL, D_FF), D_MODEL),
        'ffn1_w_down': w(ks[4], (DEPTH, D_FF, D_MODEL), D_FF),
        'mix_norm': gain(ks[5], (DEPTH, D_MODEL)),
        'w_in': w(ks[6], (DEPTH, D_MODEL, PROJ_WIDTH), D_MODEL),
        'a_q_norm': gain(ks[7], (DEPTH, HEAD_DIM)),
        'a_k_norm': gain(ks[8], (DEPTH, HEAD_DIM)),
        'b_q_norm': gain(ks[9], (DEPTH, HEAD_DIM)),
        'b_k_norm': gain(ks[10], (DEPTH, HEAD_DIM)),
        'c_q_norm': gain(ks[11], (DEPTH, HEAD_DIM)),
        'c_k_norm': gain(ks[12], (DEPTH, HEAD_DIM)),
        'c_sink': 0.5 * jax.random.normal(ks[13], (DEPTH, C_HEADS), jnp.float32),
        'group_norm': gain(ks[14], (DEPTH, MIX_WIDTH)),
        'w_out': w(ks[15], (DEPTH, MIX_WIDTH, D_MODEL), MIX_WIDTH),
        'ffn2_norm': gain(ks[16], (DEPTH, D_MODEL)),
        'ffn2_w_gate': w(ks[17], (DEPTH, D_MODEL, D_FF), D_MODEL),
        'ffn2_w_up': w(ks[18], (DEPTH, D_MODEL, D_FF), D_MODEL),
        'ffn2_w_down': w(ks[19], (DEPTH, D_FF, D_MODEL), D_FF),
    }


def reference(x, ffn1_norm, ffn1_w_gate, ffn1_w_up, ffn1_w_down, mix_norm, w_in,
              a_q_norm, a_k_norm, b_q_norm, b_k_norm, c_q_norm, c_k_norm, c_sink,
              group_norm, w_out, ffn2_norm, ffn2_w_gate, ffn2_w_up, ffn2_w_down):
    s = x.shape[1]
    rows = s // GRID_W
    t = jnp.arange(s)
    row = jnp.repeat(jnp.arange(rows), GRID_W)
    col = jnp.tile(jnp.arange(GRID_W), rows)
    cos1, sin1 = rope_tables(t, HEAD_DIM)
    cos_r, sin_r = rope_tables(row, HEAD_DIM // 2)
    cos_c, sin_c = rope_tables(col, HEAD_DIM // 2)
    for l in range(DEPTH):
        x = half_swiglu(x, ffn1_norm[l], ffn1_w_gate[l], ffn1_w_up[l], ffn1_w_down[l])
        x = x + hybrid_mixer(x, mix_norm[l], w_in[l], a_q_norm[l], a_k_norm[l],
                             b_q_norm[l], b_k_norm[l], c_q_norm[l], c_k_norm[l], c_sink[l],
                             group_norm[l], w_out[l],
                             cos1, sin1, cos_r, sin_r, cos_c, sin_c)
        x = half_swiglu(x, ffn2_norm[l], ffn2_w_gate[l], ffn2_w_up[l], ffn2_w_down[l])
    return x
```

```python
import functools
import math

import numpy as np
import jax
import jax.numpy as jnp
from jax import lax
from jax.experimental import pallas as pl
from jax.experimental.pallas import tpu as pltpu

HEAD_DIM = 64
HALF = HEAD_DIM // 2
PAIR = 2 * HEAD_DIM
A_HEADS, B_HEADS, B_KV_HEADS, C_HEADS, C_KV_HEADS = 8, 4, 2, 4, 2
DILATIONS = (1, 4, 16)
A_RADIUS = 64
C_RADIUS = 128
GRID_W = 64
ROPE_THETA = 10000.0
EPS = 1e-6
MASK_VALUE = -1e30
VMEM_LIMIT_BYTES = 56 * 1024 * 1024

ROW_TILE = 512
BAND_TILE = 128
DENSE_TILE = 256

F32 = jnp.float32
BF16 = jnp.bfloat16


def _lane_dims(axial):
    lane = np.arange(PAIR)
    half, idx = lane // HEAD_DIM, lane % HALF
    if not axial:
        return half * HALF + idx
    quarter = HALF // 2
    return np.where(idx < quarter, half * quarter + idx, HALF + half * quarter + (idx - quarter))


def _lane_slot():
    lane = np.arange(PAIR)
    return (lane % HEAD_DIM) // HALF


def _pair_columns(n_pairs, heads_of_pair, axial):
    dims, slot = _lane_dims(axial), _lane_slot()
    cols = [np.asarray(heads_of_pair(p))[slot] * HEAD_DIM + dims for p in range(n_pairs)]
    return np.concatenate(cols)


def _mha_pairs(p):
    return (2 * p, 2 * p + 1)


def _dup_pairs(p):
    return (p, p)


def _rope_dim_tables(pos, dim):
    inv_freq = 1.0 / (ROPE_THETA ** (jnp.arange(0, dim, 2, dtype=F32) / dim))
    ang = pos.astype(F32)[:, None] * inv_freq[None, :]
    ang = jnp.concatenate([ang, ang], axis=-1)
    sign = jnp.concatenate([-jnp.ones((dim // 2,), F32), jnp.ones((dim // 2,), F32)])
    return jnp.cos(ang), jnp.sin(ang) * sign[None, :]


def _rope_lane_tables(seq_len):
    t = jnp.arange(seq_len)
    cos1, sin1 = _rope_dim_tables(t, HEAD_DIM)
    cos_r, sin_r = _rope_dim_tables(t // GRID_W, HALF)
    cos_c, sin_c = _rope_dim_tables(t % GRID_W, HALF)
    cos_x = jnp.concatenate([cos_r, cos_c], axis=-1)
    sin_x = jnp.concatenate([sin_r, sin_c], axis=-1)
    d1, dx = _lane_dims(False), _lane_dims(True)
    return cos1[:, d1], sin1[:, d1], cos_x[:, dx], sin_x[:, dx]


def _head_sum_matrix():
    slot = np.concatenate([_lane_slot(), 2 + _lane_slot()])
    return jnp.asarray(slot[:, None] == slot[None, :], dtype=BF16)


def _rms(x):
    return x * lax.rsqrt(jnp.mean(x * x, axis=-1, keepdims=True) + EPS)


def _params(n_axes):
    return pltpu.CompilerParams(dimension_semantics=("arbitrary",) * n_axes,
                                vmem_limit_bytes=VMEM_LIMIT_BYTES)


def _resident(shape):
    return pl.BlockSpec(shape, lambda *_: (0,) * len(shape), pipeline_mode=pl.Buffered(1))


def _ffn_kernel(x_ref, g_ref, wg_ref, wu_ref, wd_ref, o_ref):
    x = x_ref[...]
    h = (_rms(x) * g_ref[...]).astype(BF16)
    gate = jnp.dot(h, wg_ref[...], preferred_element_type=F32)
    up = jnp.dot(h, wu_ref[...], preferred_element_type=F32)
    act = (gate * (1.0 / (1.0 + jnp.exp(-gate))) * up).astype(BF16)
    o_ref[...] = x + 0.5 * jnp.dot(act, wd_ref[...], preferred_element_type=F32)


def _half_swiglu(x, g, wg, wu, wd):
    n, d = x.shape
    f = wg.shape[1]
    row = pl.BlockSpec((ROW_TILE, d), lambda i: (i, 0))
    return pl.pallas_call(
        _ffn_kernel,
        grid=(n // ROW_TILE,),
        in_specs=[row, _resident((1, d)), _resident((d, f)), _resident((d, f)), _resident((f, d))],
        out_specs=row,
        out_shape=jax.ShapeDtypeStruct((n, d), F32),
        compiler_params=_params(1),
    )(x, g.reshape(1, d), wg, wu, wd)


_SEG = {"aq": (0, 512), "ak": (512, 512), "av": (1024, 512),
        "bq": (1536, 256), "bk": (1792, 256), "bv": (2048, 256),
        "cq": (2304, 256), "ck": (2560, 256), "cv": (2816, 256)}
_SEG_ORDER = ("aq", "ak", "av", "bq", "bk", "bv", "cq", "ck", "cv")
PROJ_COLS = 3072


def _norm_rope(seg, head_sum, gain, cos, sin):
    sq = seg * seg
    hi = sq.astype(BF16)
    lo = (sq - hi.astype(F32)).astype(BF16)
    ss = (jnp.dot(hi, head_sum, preferred_element_type=F32)
          + jnp.dot(lo, head_sum, preferred_element_type=F32))
    y = seg * lax.rsqrt(ss * (1.0 / HEAD_DIM) + EPS)
    out = []
    for k in range(2):
        yk = y[:, k * PAIR:(k + 1) * PAIR] * gain
        out.append(yk * cos + pltpu.roll(yk, HEAD_DIM, 1) * sin)
    return jnp.concatenate(out, axis=1)


def _proj_kernel(x_ref, g_ref, w_ref, hs_ref, gain_ref, cos1_ref, sin1_ref, cosx_ref, sinx_ref,
                 *out_refs):
    outs = dict(zip(_SEG_ORDER, out_refs))
    h = (_rms(x_ref[...]) * g_ref[...]).astype(BF16)
    proj = jnp.dot(h, w_ref[...], preferred_element_type=F32)
    head_sum = hs_ref[...]
    tables = {"a": (cos1_ref[...], sin1_ref[...]), "b": (cosx_ref[...], sinx_ref[...]),
              "c": (cos1_ref[...], sin1_ref[...])}
    gain_row = {"aq": 0, "ak": 1, "bq": 2, "bk": 3, "cq": 4, "ck": 5}
    for name in _SEG_ORDER:
        start, width = _SEG[name]
        if name in gain_row:
            gain = gain_ref[gain_row[name]:gain_row[name] + 1, :]
            cos, sin = tables[name[0]]
            for c in range(width // 256):
                seg = proj[:, start + c * 256:start + (c + 1) * 256]
                outs[name][:, c * 256:(c + 1) * 256] = _norm_rope(seg, head_sum, gain, cos, sin).astype(BF16)
        else:
            outs[name][...] = proj[:, start:start + width].astype(BF16)


def _project(x, g, w, head_sum, gains, tables, seq_len):
    n, d = x.shape
    blocks_per_seq = seq_len // ROW_TILE
    row = pl.BlockSpec((ROW_TILE, d), lambda i: (i, 0))
    table = pl.BlockSpec((ROW_TILE, PAIR), lambda i: (i % blocks_per_seq, 0))
    out_specs = [pl.BlockSpec((ROW_TILE, _SEG[s][1]), lambda i: (i, 0)) for s in _SEG_ORDER]
    out_shape = [jax.ShapeDtypeStruct((n, _SEG[s][1]), BF16) for s in _SEG_ORDER]
    return pl.pallas_call(
        _proj_kernel,
        grid=(n // ROW_TILE,),
        in_specs=[row, _resident((1, d)), _resident((d, PROJ_COLS)), _resident((256, 256)),
                  _resident(gains.shape), table, table, table, table],
        out_specs=out_specs,
        out_shape=out_shape,
        compiler_params=_params(1),
    )(x, g.reshape(1, d), w, head_sum, gains, *tables)


def _stack_heads(q):
    own = _own_lanes(q.shape)
    zero = jnp.zeros_like(q)
    return jnp.concatenate([jnp.where(own, q, zero), jnp.where(own, zero, q)], axis=0)


def _own_lanes(shape):
    lane = lax.broadcasted_iota(jnp.int32, shape, 1)
    return (lane & HALF) == 0


def _unstack_heads(x2):
    rows = x2.shape[0] // 2
    return jnp.where(_own_lanes((rows, PAIR)), x2[:rows], x2[rows:])


def _band_kernel(*refs, radius, n_blocks, has_sink, want_lse):
    refs = list(refs)
    sink_ref = refs.pop(0) if has_sink else None
    q_ref, kp_ref, kc_ref, kn_ref, vp_ref, vc_ref, vn_ref, o_ref = refs[:8]
    lse_ref = refs[8] if want_lse else None
    tq = q_ref.shape[0]
    i = pl.program_id(3)
    q2 = _stack_heads(q_ref[...])
    k = jnp.concatenate([kp_ref[...], kc_ref[...], kn_ref[...]], axis=0)
    v = jnp.concatenate([vp_ref[...], vc_ref[...], vn_ref[...]], axis=0)
    s = lax.dot_general(q2, k, (((1,), (1,)), ((), ())), preferred_element_type=F32)
    row = lax.broadcasted_iota(jnp.int32, s.shape, 0)
    col = lax.broadcasted_iota(jnp.int32, s.shape, 1)
    rel = col - tq - (row & (tq - 1))
    valid = (jnp.abs(rel) <= radius) & ((col >= tq) | (i > 0)) & ((col < 2 * tq) | (i < n_blocks - 1))
    s = jnp.where(valid, s, MASK_VALUE)
    m = jnp.max(s, axis=1, keepdims=True)
    if has_sink:
        pair = pl.program_id(2)
        first = lax.broadcasted_iota(jnp.int32, m.shape, 0) < tq
        sink = jnp.where(first, sink_ref[2 * pair], sink_ref[2 * pair + 1])
        m = jnp.maximum(m, sink)
    p = jnp.exp(s - m)
    denom = jnp.sum(p, axis=1, keepdims=True)
    if has_sink:
        denom = denom + jnp.exp(sink - m)
    o2 = jnp.dot(p.astype(BF16), v, preferred_element_type=F32) / denom
    o_ref[...] = _unstack_heads(o2)
    if want_lse:
        lse_ref[...] = _unstack_heads(jnp.broadcast_to(m + jnp.log(denom), o2.shape))


def _banded_attention(q, k, v, dilation, radius, sink=None, want_lse=False):
    batch, seq_len, width = q.shape
    n_pairs = width // PAIR
    length = seq_len // dilation
    n_blocks = length // BAND_TILE
    view = lambda t: t.reshape(batch, length, dilation * width)

    def spec(shift):
        def index(b, r, p, i):
            return (b, jnp.clip(i + shift, 0, n_blocks - 1), r * n_pairs + p)
        return pl.BlockSpec((None, BAND_TILE, PAIR), index)

    in_specs = [spec(0), spec(-1), spec(0), spec(1), spec(-1), spec(0), spec(1)]
    args = [view(q), view(k), view(k), view(k), view(v), view(v), view(v)]
    if sink is not None:
        in_specs.insert(0, pl.BlockSpec(memory_space=pltpu.SMEM))
        args.insert(0, sink.astype(F32))
    out_sds = jax.ShapeDtypeStruct((batch, length, dilation * width), F32)
    out = pl.pallas_call(
        functools.partial(_band_kernel, radius=radius, n_blocks=n_blocks,
                          has_sink=sink is not None, want_lse=want_lse),
        grid=(batch, dilation, n_pairs, n_blocks),
        in_specs=in_specs,
        out_specs=[spec(0), spec(0)] if want_lse else spec(0),
        out_shape=[out_sds, out_sds] if want_lse else out_sds,
        compiler_params=_params(4),
    )(*args)
    if want_lse:
        return tuple(t.reshape(batch, seq_len, width) for t in out)
    return out.reshape(batch, seq_len, width)


def _dense_kernel(q_ref, k_ref, v_ref, o_ref):
    q2 = _stack_heads(q_ref[...])
    s = lax.dot_general(q2, k_ref[...], (((1,), (1,)), ((), ())), preferred_element_type=F32)
    m = jnp.max(s, axis=1, keepdims=True)
    p = jnp.exp(s - m)
    denom = jnp.sum(p, axis=1, keepdims=True)
    o2 = jnp.dot(p.astype(BF16), v_ref[...], preferred_element_type=F32) / denom
    o_ref[...] = _unstack_heads(o2)


def _dense_attention(q, k, v):
    batch, seq_len, width = q.shape
    n_pairs = width // PAIR
    q_spec = pl.BlockSpec((None, DENSE_TILE, PAIR), lambda b, p, i: (b, i, p))
    kv_spec = pl.BlockSpec((None, seq_len, PAIR), lambda b, p, i: (b, 0, p))
    return pl.pallas_call(
        _dense_kernel,
        grid=(batch, n_pairs, seq_len // DENSE_TILE),
        in_specs=[q_spec, kv_spec, kv_spec],
        out_specs=q_spec,
        out_shape=jax.ShapeDtypeStruct(q.shape, F32),
        compiler_params=_params(3),
    )(q, k, v)


def _out_kernel(x_ref, o1_ref, o4_ref, o16_ref, l1_ref, l4_ref, l16_ref, ob_ref, oc_ref,
                g_ref, w_ref, y_ref):
    l1, l4, l16 = l1_ref[...], l4_ref[...], l16_ref[...]
    top = jnp.maximum(jnp.maximum(l1, l4), l16)
    e1, e4, e16 = jnp.exp(l1 - top), jnp.exp(l4 - top), jnp.exp(l16 - top)
    oa = (e1 * o1_ref[...] + e4 * o4_ref[...] + e16 * o16_ref[...]) / (e1 + e4 + e16)
    g = g_ref[...]
    wa, wb = oa.shape[1], ob_ref.shape[1]
    mixed = jnp.concatenate([_rms(oa) * g[:, :wa],
                             _rms(ob_ref[...]) * g[:, wa:wa + wb],
                             _rms(oc_ref[...]) * g[:, wa + wb:]], axis=1)
    y_ref[...] = x_ref[...] + jnp.dot(mixed.astype(BF16), w_ref[...], preferred_element_type=F32)


def _mix_out(x, oa_parts, lse_parts, ob, oc, g, w):
    n, d = x.shape
    rows = lambda width: pl.BlockSpec((ROW_TILE, width), lambda i: (i, 0))
    wa, wb, wc = oa_parts[0].shape[1], ob.shape[1], oc.shape[1]
    return pl.pallas_call(
        _out_kernel,
        grid=(n // ROW_TILE,),
        in_specs=[rows(d)] + [rows(wa)] * 6 + [rows(wb), rows(wc), _resident((1, d)), _resident(w.shape)],
        out_specs=rows(d),
        out_shape=jax.ShapeDtypeStruct((n, d), F32),
        compiler_params=_params(1),
    )(x, *oa_parts, *lse_parts, ob, oc, g.reshape(1, d), w)


def _mixer_columns():
    a = _pair_columns(A_HEADS // 2, _mha_pairs, False)
    b_q = _pair_columns(B_HEADS // 2, _mha_pairs, True)
    b_kv = _pair_columns(B_KV_HEADS, _dup_pairs, True)
    c_q = _pair_columns(C_HEADS // 2, _mha_pairs, False)
    c_kv = _pair_columns(C_KV_HEADS, _dup_pairs, False)
    sizes = [A_HEADS * 64] * 3 + [B_HEADS * 64, B_KV_HEADS * 64, B_KV_HEADS * 64,
                                  C_HEADS * 64, C_KV_HEADS * 64, C_KV_HEADS * 64]
    starts = np.concatenate([[0], np.cumsum(sizes)[:-1]])
    in_cols = np.concatenate([s + c for s, c in zip(starts, (a, a, a, b_q, b_kv, b_kv, c_q, c_kv, c_kv))])
    wa, wb = A_HEADS * 64, B_HEADS * 64
    out_rows = np.concatenate([a, wa + b_q, wa + wb + c_q])
    return in_cols, out_rows


def kernel(x, ffn1_norm, ffn1_w_gate, ffn1_w_up, ffn1_w_down, mix_norm, w_in, a_q_norm, a_k_norm, b_q_norm, b_k_norm, c_q_norm, c_k_norm, c_sink, group_norm, w_out, ffn2_norm, ffn2_w_gate, ffn2_w_up, ffn2_w_down):
    batch, seq_len, d = x.shape
    assert seq_len % ROW_TILE == 0 and seq_len % DENSE_TILE == 0
    assert all(seq_len % (dil * BAND_TILE) == 0 for dil in DILATIONS)
    depth = w_in.shape[0]
    in_cols, out_rows = _mixer_columns()
    tables = _rope_lane_tables(seq_len)
    head_sum = _head_sum_matrix()
    d1, dx = _lane_dims(False), _lane_dims(True)
    q_scale = 1.0 / math.sqrt(HEAD_DIM)
    bf = lambda t: t.astype(BF16)
    shape3 = lambda t: t.reshape(batch, seq_len, t.shape[-1])

    xf = x.reshape(batch * seq_len, d)
    for l in range(depth):
        xf = _half_swiglu(xf, ffn1_norm[l], bf(ffn1_w_gate[l]), bf(ffn1_w_up[l]), bf(ffn1_w_down[l]))
        gains = jnp.stack([a_q_norm[l][d1] * q_scale, a_k_norm[l][d1],
                           b_q_norm[l][dx] * q_scale, b_k_norm[l][dx],
                           c_q_norm[l][d1] * q_scale, c_k_norm[l][d1]]).astype(F32)
        aq, ak, av, bq, bk, bv, cq, ck, cv = map(shape3, _project(
            xf, mix_norm[l], bf(w_in[l][:, in_cols]), head_sum, gains, tables, seq_len))
        branches = [_banded_attention(aq, ak, av, dil, A_RADIUS, want_lse=True) for dil in DILATIONS]
        ob = _dense_attention(bq, bk, bv)
        oc = _banded_attention(cq, ck, cv, 1, C_RADIUS, sink=c_sink[l])
        flat = lambda t: t.reshape(batch * seq_len, t.shape[-1])
        xf = _mix_out(xf, [flat(o) for o, _ in branches], [flat(s) for _, s in branches],
                      flat(ob), flat(oc), group_norm[l][out_rows], bf(w_out[l][out_rows, :]))
        xf = _half_swiglu(xf, ffn2_norm[l], bf(ffn2_w_gate[l]), bf(ffn2_w_up[l]), bf(ffn2_w_down[l]))
    return xf.reshape(batch, seq_len, d)
```

```python
import functools
import math

import numpy as np
import jax
import jax.numpy as jnp
from jax import lax
from jax.experimental import pallas as pl
from jax.experimental.pallas import tpu as pltpu

HEAD_DIM = 64
HALF = HEAD_DIM // 2
PAIR = 2 * HEAD_DIM
A_HEADS, B_HEADS, B_KV_HEADS, C_HEADS, C_KV_HEADS = 8, 4, 2, 4, 2
DILATIONS = (1, 4, 16)
A_RADIUS = 64
C_RADIUS = 128
GRID_W = 64
ROPE_THETA = 10000.0
EPS = 1e-6
MASK_VALUE = -1e30
VMEM_LIMIT_BYTES = 56 * 1024 * 1024

ROW_TILE = 512
BAND_TILE = 128
BAND_UNROLL = 4
DENSE_TILE = 256
DENSE_KEYS = 512

F32 = jnp.float32
BF16 = jnp.bfloat16


def _lane_dims(axial):
    lane = np.arange(PAIR)
    half, idx = lane // HEAD_DIM, lane % HALF
    if not axial:
        return half * HALF + idx
    quarter = HALF // 2
    return np.where(idx < quarter, half * quarter + idx, HALF + half * quarter + (idx - quarter))


def _lane_slot():
    lane = np.arange(PAIR)
    return (lane % HEAD_DIM) // HALF


def _pair_columns(n_pairs, heads_of_pair, axial):
    dims, slot = _lane_dims(axial), _lane_slot()
    cols = [np.asarray(heads_of_pair(p))[slot] * HEAD_DIM + dims for p in range(n_pairs)]
    return np.concatenate(cols)


def _mha_pairs(p):
    return (2 * p, 2 * p + 1)


def _dup_pairs(p):
    return (p, p)


def _rope_dim_tables(pos, dim):
    inv_freq = 1.0 / (ROPE_THETA ** (jnp.arange(0, dim, 2, dtype=F32) / dim))
    ang = pos.astype(F32)[:, None] * inv_freq[None, :]
    ang = jnp.concatenate([ang, ang], axis=-1)
    sign = jnp.concatenate([-jnp.ones((dim // 2,), F32), jnp.ones((dim // 2,), F32)])
    return jnp.cos(ang), jnp.sin(ang) * sign[None, :]


def _rope_lane_tables(seq_len):
    t = jnp.arange(seq_len)
    cos1, sin1 = _rope_dim_tables(t, HEAD_DIM)
    cos_r, sin_r = _rope_dim_tables(t // GRID_W, HALF)
    cos_c, sin_c = _rope_dim_tables(t % GRID_W, HALF)
    cos_x = jnp.concatenate([cos_r, cos_c], axis=-1)
    sin_x = jnp.concatenate([sin_r, sin_c], axis=-1)
    d1, dx = _lane_dims(False), _lane_dims(True)
    return cos1[:, d1], sin1[:, d1], cos_x[:, dx], sin_x[:, dx]


def _head_sum_matrix():
    slot = np.concatenate([_lane_slot(), 2 + _lane_slot()])
    return jnp.asarray(slot[:, None] == slot[None, :], dtype=BF16)


def _window_bias(radius):
    window = BAND_TILE + 2 * radius
    row = np.arange(2 * BAND_TILE)[:, None] % BAND_TILE
    col = np.arange(window)[None, :]
    bias = [np.where(np.abs(col - off - row) <= radius, 0.0, MASK_VALUE) for off in (0, radius, 2 * radius)]
    return jnp.asarray(np.stack(bias), dtype=F32)


def _rms(x):
    return x * lax.rsqrt(jnp.mean(x * x, axis=-1, keepdims=True) + EPS)


def _params(n_axes):
    return pltpu.CompilerParams(dimension_semantics=("arbitrary",) * n_axes,
                                vmem_limit_bytes=VMEM_LIMIT_BYTES)


def _resident(shape):
    return pl.BlockSpec(shape, lambda *_: (0,) * len(shape), pipeline_mode=pl.Buffered(1))


def _ffn_kernel(x_ref, g_ref, wg_ref, wu_ref, wd_ref, o_ref):
    x = x_ref[...]
    h = (_rms(x) * g_ref[...]).astype(BF16)
    gate = jnp.dot(h, wg_ref[...], preferred_element_type=F32)
    up = jnp.dot(h, wu_ref[...], preferred_element_type=F32)
    act = (gate * (1.0 / (1.0 + jnp.exp(-gate))) * up).astype(BF16)
    o_ref[...] = x + 0.5 * jnp.dot(act, wd_ref[...], preferred_element_type=F32)


def _half_swiglu(x, g, wg, wu, wd):
    n, d = x.shape
    f = wg.shape[1]
    row = pl.BlockSpec((ROW_TILE, d), lambda i: (i, 0))
    return pl.pallas_call(
        _ffn_kernel,
        grid=(n // ROW_TILE,),
        in_specs=[row, _resident((1, d)), _resident((d, f)), _resident((d, f)), _resident((f, d))],
        out_specs=row,
        out_shape=jax.ShapeDtypeStruct((n, d), F32),
        compiler_params=_params(1),
    )(x, g.reshape(1, d), wg, wu, wd)


_SEG = {"aq": (0, 512), "ak": (512, 512), "av": (1024, 512),
        "bq": (1536, 256), "bk": (1792, 256), "bv": (2048, 256),
        "cq": (2304, 256), "ck": (2560, 256), "cv": (2816, 256)}
_SEG_ORDER = ("aq", "ak", "av", "bq", "bk", "bv", "cq", "ck", "cv")
PROJ_COLS = 3072


def _seg_dtype(name):
    return F32 if name[0] == "a" else BF16


def _norm_rope(seg, head_sum, gain, cos, sin):
    sq = seg * seg
    hi = sq.astype(BF16)
    lo = (sq - hi.astype(F32)).astype(BF16)
    ss = (jnp.dot(hi, head_sum, preferred_element_type=F32)
          + jnp.dot(lo, head_sum, preferred_element_type=F32))
    y = seg * lax.rsqrt(ss * (1.0 / HEAD_DIM) + EPS)
    out = []
    for k in range(2):
        yk = y[:, k * PAIR:(k + 1) * PAIR] * gain
        out.append(yk * cos + pltpu.roll(yk, HEAD_DIM, 1) * sin)
    return jnp.concatenate(out, axis=1)


def _proj_kernel(x_ref, g_ref, w_ref, hs_ref, gain_ref, cos1_ref, sin1_ref, cosx_ref, sinx_ref,
                 *out_refs):
    outs = dict(zip(_SEG_ORDER, out_refs))
    h = (_rms(x_ref[...]) * g_ref[...]).astype(BF16)
    proj = jnp.dot(h, w_ref[...], preferred_element_type=F32)
    head_sum = hs_ref[...]
    tables = {"a": (cos1_ref[...], sin1_ref[...]), "b": (cosx_ref[...], sinx_ref[...]),
              "c": (cos1_ref[...], sin1_ref[...])}
    gain_row = {"aq": 0, "ak": 1, "bq": 2, "bk": 3, "cq": 4, "ck": 5}
    for name in _SEG_ORDER:
        start, width = _SEG[name]
        out = outs[name]
        if name in gain_row:
            gain = gain_ref[gain_row[name]:gain_row[name] + 1, :]
            cos, sin = tables[name[0]]
            for c in range(width // 256):
                seg = proj[:, start + c * 256:start + (c + 1) * 256]
                out[:, c * 256:(c + 1) * 256] = _norm_rope(seg, head_sum, gain, cos, sin).astype(out.dtype)
        else:
            out[...] = proj[:, start:start + width].astype(out.dtype)


def _project(x, g, w, head_sum, gains, tables, seq_len):
    n, d = x.shape
    blocks_per_seq = seq_len // ROW_TILE
    row = pl.BlockSpec((ROW_TILE, d), lambda i: (i, 0))
    table = pl.BlockSpec((ROW_TILE, PAIR), lambda i: (i % blocks_per_seq, 0))
    out_specs = [pl.BlockSpec((ROW_TILE, _SEG[s][1]), lambda i: (i, 0)) for s in _SEG_ORDER]
    out_shape = [jax.ShapeDtypeStruct((n, _SEG[s][1]), _seg_dtype(s)) for s in _SEG_ORDER]
    return pl.pallas_call(
        _proj_kernel,
        grid=(n // ROW_TILE,),
        in_specs=[row, _resident((1, d)), _resident((d, PROJ_COLS)), _resident((256, 256)),
                  _resident(gains.shape), table, table, table, table],
        out_specs=out_specs,
        out_shape=out_shape,
        compiler_params=_params(1),
    )(x, g.reshape(1, d), w, head_sum, gains, *tables)


def _own_lanes(shape):
    lane = lax.broadcasted_iota(jnp.int32, shape, 1)
    return (lane & HALF) == 0


def _stack_heads(q):
    own = _own_lanes(q.shape)
    zero = jnp.zeros_like(q)
    return jnp.concatenate([jnp.where(own, q, zero), jnp.where(own, zero, q)], axis=0)


def _unstack_heads(x2):
    rows = x2.shape[0] // 2
    return jnp.where(_own_lanes((rows, PAIR)), x2[:rows], x2[rows:])


def _window_kernel(*refs, dilations, radius, has_sink):
    refs = list(refs)
    sink_ref = refs.pop(0) if has_sink else None
    q_ref, k_ref, v_ref, bias_ref, o_ref = refs[:5]
    u_acc, m_acc, l_acc = refs[5:] if len(dilations) > 1 else (None,) * 3
    seq_len = q_ref.shape[0]
    tq, window = BAND_TILE, BAND_TILE + 2 * radius
    ones = jnp.ones((window, PAIR), BF16)
    if has_sink:
        pair = pl.program_id(1)
        first_head = lax.broadcasted_iota(jnp.int32, (2 * tq, 1), 0) < tq
        sink = jnp.where(first_head, sink_ref[2 * pair], sink_ref[2 * pair + 1])

    for d_index, dil in enumerate(dilations):
        length = seq_len // dil
        n_tiles = length // tq
        first, last = d_index == 0, d_index == len(dilations) - 1

        def rows(start, size, res, dil=dil):
            if dil == 1:
                return pl.ds(pl.multiple_of(start, A_RADIUS), size)
            return pl.ds(start * dil + res, size, stride=dil)

        def tile(idx, carry, n_tiles=n_tiles, length=length, first=first, last=last, rows=rows):
            res, i = lax.div(idx, n_tiles), lax.rem(idx, n_tiles)
            q_start = i * tq
            k_start = jnp.clip(q_start - radius, 0, length - window)
            variant = jnp.where(i == 0, 0, jnp.where(i == n_tiles - 1, 2, 1))
            q = q_ref[rows(q_start, tq, res), :].astype(BF16)
            k = k_ref[rows(k_start, window, res), :].astype(BF16)
            v = v_ref[rows(k_start, window, res), :].astype(BF16)
            s = lax.dot_general(_stack_heads(q), k, (((1,), (1,)), ((), ())),
                                preferred_element_type=F32) + bias_ref[variant]
            m2 = jnp.max(s, axis=1, keepdims=True)
            if has_sink:
                m2 = jnp.maximum(m2, sink)
            p = jnp.exp(s - m2).astype(BF16)
            ul = jnp.dot(p, jnp.concatenate([v, ones], axis=1), preferred_element_type=F32)
            l2 = ul[:, PAIR:]
            if has_sink:
                l2 = l2 + jnp.exp(sink - m2)
            u, l = _unstack_heads(ul[:, :PAIR]), _unstack_heads(l2)
            m = _unstack_heads(jnp.broadcast_to(m2, (2 * tq, PAIR)))
            out_rows = rows(q_start, tq, res)
            if not first:
                m_old = m_acc[out_rows, :]
                m_new = jnp.maximum(m_old, m)
                w_old, w_new = jnp.exp(m_old - m_new), jnp.exp(m - m_new)
                u = w_old * u_acc[out_rows, :] + w_new * u
                l = w_old * l_acc[out_rows, :] + w_new * l
                m = m_new
            if last:
                o_ref[out_rows, :] = (u / l).astype(o_ref.dtype)
            else:
                u_acc[out_rows, :] = u
                m_acc[out_rows, :] = m
                l_acc[out_rows, :] = l
            return carry

        lax.fori_loop(0, dil * n_tiles, tile, 0, unroll=BAND_UNROLL)


def _window_attention(q, k, v, dilations, radius, sink=None):
    batch, seq_len, width = q.shape
    window = BAND_TILE + 2 * radius
    assert all(seq_len % (dil * BAND_TILE) == 0 and seq_len // dil >= window for dil in dilations)
    bias = _window_bias(radius)
    seq = pl.BlockSpec((None, seq_len, PAIR), lambda b, p: (b, 0, p))
    in_specs = [seq, seq, seq, _resident(bias.shape)]
    args = [q, k, v, bias]
    if sink is not None:
        in_specs.insert(0, pl.BlockSpec(memory_space=pltpu.SMEM))
        args.insert(0, sink.astype(F32))
    return pl.pallas_call(
        functools.partial(_window_kernel, dilations=dilations, radius=radius, has_sink=sink is not None),
        grid=(batch, width // PAIR),
        in_specs=in_specs,
        out_specs=seq,
        out_shape=jax.ShapeDtypeStruct(q.shape, F32),
        scratch_shapes=[pltpu.VMEM((seq_len, PAIR), F32)] * (3 if len(dilations) > 1 else 0),
        compiler_params=_params(2),
    )(*args)


def _dense_kernel(q_ref, k_ref, v_ref, o_ref, v1_ref):
    seq_len = k_ref.shape[0]

    @pl.when(pl.program_id(2) == 0)
    def _():
        v1_ref[:, :PAIR] = v_ref[...]
        v1_ref[:, PAIR:] = jnp.ones((seq_len, PAIR), BF16)

    q2 = _stack_heads(q_ref[...])
    m = acc = None
    for start in range(0, seq_len, DENSE_KEYS):
        keys = slice(start, start + DENSE_KEYS)
        s = lax.dot_general(q2, k_ref[keys, :], (((1,), (1,)), ((), ())), preferred_element_type=F32)
        m_chunk = jnp.max(s, axis=1, keepdims=True)
        m_new = m_chunk if m is None else jnp.maximum(m, m_chunk)
        pv = jnp.dot(jnp.exp(s - m_new).astype(BF16), v1_ref[keys, :], preferred_element_type=F32)
        acc = pv if m is None else jnp.exp(m - m_new) * acc + pv
        m = m_new
    o_ref[...] = _unstack_heads(acc[:, :PAIR] / acc[:, PAIR:])


def _dense_attention(q, k, v):
    batch, seq_len, width = q.shape
    n_pairs = width // PAIR
    q_spec = pl.BlockSpec((None, DENSE_TILE, PAIR), lambda b, p, i: (b, i, p))
    kv_spec = pl.BlockSpec((None, seq_len, PAIR), lambda b, p, i: (b, 0, p))
    return pl.pallas_call(
        _dense_kernel,
        grid=(batch, n_pairs, seq_len // DENSE_TILE),
        in_specs=[q_spec, kv_spec, kv_spec],
        out_specs=q_spec,
        out_shape=jax.ShapeDtypeStruct(q.shape, F32),
        scratch_shapes=[pltpu.VMEM((seq_len, 2 * PAIR), BF16)],
        compiler_params=_params(3),
    )(q, k, v)


def _out_kernel(x_ref, oa_ref, ob_ref, oc_ref, g_ref, w_ref, y_ref):
    g = g_ref[...]
    wa, wb = oa_ref.shape[1], ob_ref.shape[1]
    mixed = jnp.concatenate([_rms(oa_ref[...]) * g[:, :wa],
                             _rms(ob_ref[...]) * g[:, wa:wa + wb],
                             _rms(oc_ref[...]) * g[:, wa + wb:]], axis=1)
    y_ref[...] = x_ref[...] + jnp.dot(mixed.astype(BF16), w_ref[...], preferred_element_type=F32)


def _mix_out(x, oa, ob, oc, g, w):
    n, d = x.shape
    rows = lambda width: pl.BlockSpec((ROW_TILE, width), lambda i: (i, 0))
    return pl.pallas_call(
        _out_kernel,
        grid=(n // ROW_TILE,),
        in_specs=[rows(d), rows(oa.shape[1]), rows(ob.shape[1]), rows(oc.shape[1]),
                  _resident((1, d)), _resident(w.shape)],
        out_specs=rows(d),
        out_shape=jax.ShapeDtypeStruct((n, d), F32),
        compiler_params=_params(1),
    )(x, oa, ob, oc, g.reshape(1, d), w)


def _mixer_columns():
    a = _pair_columns(A_HEADS // 2, _mha_pairs, False)
    b_q = _pair_columns(B_HEADS // 2, _mha_pairs, True)
    b_kv = _pair_columns(B_KV_HEADS, _dup_pairs, True)
    c_q = _pair_columns(C_HEADS // 2, _mha_pairs, False)
    c_kv = _pair_columns(C_KV_HEADS, _dup_pairs, False)
    sizes = [A_HEADS * 64] * 3 + [B_HEADS * 64, B_KV_HEADS * 64, B_KV_HEADS * 64,
                                  C_HEADS * 64, C_KV_HEADS * 64, C_KV_HEADS * 64]
    starts = np.concatenate([[0], np.cumsum(sizes)[:-1]])
    in_cols = np.concatenate([s + c for s, c in zip(starts, (a, a, a, b_q, b_kv, b_kv, c_q, c_kv, c_kv))])
    wa, wb = A_HEADS * 64, B_HEADS * 64
    out_rows = np.concatenate([a, wa + b_q, wa + wb + c_q])
    return in_cols, out_rows


def kernel(x, ffn1_norm, ffn1_w_gate, ffn1_w_up, ffn1_w_down, mix_norm, w_in, a_q_norm, a_k_norm, b_q_norm, b_k_norm, c_q_norm, c_k_norm, c_sink, group_norm, w_out, ffn2_norm, ffn2_w_gate, ffn2_w_up, ffn2_w_down):
    batch, seq_len, d = x.shape
    assert seq_len % ROW_TILE == 0 and seq_len % DENSE_TILE == 0 and seq_len % DENSE_KEYS == 0
    depth = w_in.shape[0]
    in_cols, out_rows = _mixer_columns()
    tables = _rope_lane_tables(seq_len)
    head_sum = _head_sum_matrix()
    d1, dx = _lane_dims(False), _lane_dims(True)
    q_scale = 1.0 / math.sqrt(HEAD_DIM)
    bf = lambda t: t.astype(BF16)
    shape3 = lambda t: t.reshape(batch, seq_len, t.shape[-1])
    flat = lambda t: t.reshape(batch * seq_len, t.shape[-1])

    xf = x.reshape(batch * seq_len, d)
    for l in range(depth):
        xf = _half_swiglu(xf, ffn1_norm[l], bf(ffn1_w_gate[l]), bf(ffn1_w_up[l]), bf(ffn1_w_down[l]))
        gains = jnp.stack([a_q_norm[l][d1] * q_scale, a_k_norm[l][d1],
                           b_q_norm[l][dx] * q_scale, b_k_norm[l][dx],
                           c_q_norm[l][d1] * q_scale, c_k_norm[l][d1]]).astype(F32)
        aq, ak, av, bq, bk, bv, cq, ck, cv = map(shape3, _project(
            xf, mix_norm[l], bf(w_in[l][:, in_cols]), head_sum, gains, tables, seq_len))
        oa = _window_attention(aq, ak, av, DILATIONS, A_RADIUS)
        ob = _dense_attention(bq, bk, bv)
        oc = _window_attention(cq, ck, cv, (1,), C_RADIUS, sink=c_sink[l])
        xf = _mix_out(xf, flat(oa), flat(ob), flat(oc), group_norm[l][out_rows], bf(w_out[l][out_rows, :]))
        xf = _half_swiglu(xf, ffn2_norm[l], bf(ffn2_w_gate[l]), bf(ffn2_w_up[l]), bf(ffn2_w_down[l]))
    return xf.reshape(batch, seq_len, d)
```

```python
import functools
import math

import numpy as np
import jax
import jax.numpy as jnp
from jax import lax
from jax.experimental import pallas as pl
from jax.experimental.pallas import tpu as pltpu

HEAD_DIM = 64
HALF = HEAD_DIM // 2
PAIR = 2 * HEAD_DIM
A_HEADS, B_HEADS, B_KV_HEADS, C_HEADS, C_KV_HEADS = 8, 4, 2, 4, 2
DILATIONS = (1, 4, 16)
QUAD = 4
A_RADIUS = 64
C_RADIUS = 128
GRID_W = 64
ROPE_THETA = 10000.0
EPS = 1e-6
MASK_VALUE = -1e30
VMEM_LIMIT_BYTES = 56 * 1024 * 1024

ROW_TILE = 512
BAND_TILE = 128
BAND_UNROLL = 8
DENSE_TILE = 512
DENSE_KEYS = 512

F32 = jnp.float32
BF16 = jnp.bfloat16


def _lane_dims(axial):
    lane = np.arange(PAIR)
    half, idx = lane // HEAD_DIM, lane % HALF
    if not axial:
        return half * HALF + idx
    quarter = HALF // 2
    return np.where(idx < quarter, half * quarter + idx, HALF + half * quarter + (idx - quarter))


def _lane_slot():
    lane = np.arange(PAIR)
    return (lane % HEAD_DIM) // HALF


def _pair_columns(n_pairs, heads_of_pair, axial):
    dims, slot = _lane_dims(axial), _lane_slot()
    cols = [np.asarray(heads_of_pair(p))[slot] * HEAD_DIM + dims for p in range(n_pairs)]
    return np.concatenate(cols)


def _mha_pairs(p):
    return (2 * p, 2 * p + 1)


def _dup_pairs(p):
    return (p, p)


def _rope_dim_tables(pos, dim):
    inv_freq = 1.0 / (ROPE_THETA ** (jnp.arange(0, dim, 2, dtype=F32) / dim))
    ang = pos.astype(F32)[:, None] * inv_freq[None, :]
    ang = jnp.concatenate([ang, ang], axis=-1)
    sign = jnp.concatenate([-jnp.ones((dim // 2,), F32), jnp.ones((dim // 2,), F32)])
    return jnp.cos(ang), jnp.sin(ang) * sign[None, :]


def _rope_lane_tables(seq_len):
    t = jnp.arange(seq_len)
    cos1, sin1 = _rope_dim_tables(t, HEAD_DIM)
    cos_r, sin_r = _rope_dim_tables(t // GRID_W, HALF)
    cos_c, sin_c = _rope_dim_tables(t % GRID_W, HALF)
    cos_x = jnp.concatenate([cos_r, cos_c], axis=-1)
    sin_x = jnp.concatenate([sin_r, sin_c], axis=-1)
    d1, dx = _lane_dims(False), _lane_dims(True)
    return cos1[:, d1], sin1[:, d1], cos_x[:, dx], sin_x[:, dx]


def _head_sum_matrix():
    slot = np.concatenate([_lane_slot(), 2 + _lane_slot()])
    return jnp.asarray(slot[:, None] == slot[None, :], dtype=BF16)


def _window_bias(radius, interleave=1):
    window = BAND_TILE + 2 * radius

    def offsets(count):
        index = np.arange(count)
        return (index % (count // interleave)) * interleave + index // (count // interleave)

    row = np.tile(offsets(BAND_TILE), 2)[:, None]
    col = offsets(window)[None, :]
    bias = [np.where(np.abs(col - off - row) <= radius, 0.0, MASK_VALUE) for off in (0, radius, 2 * radius)]
    return jnp.asarray(np.stack(bias), dtype=F32)


def _rms(x):
    return x * lax.rsqrt(jnp.mean(x * x, axis=-1, keepdims=True) + EPS)


def _params(n_axes):
    return pltpu.CompilerParams(dimension_semantics=("arbitrary",) * n_axes,
                                vmem_limit_bytes=VMEM_LIMIT_BYTES)


def _resident(shape):
    return pl.BlockSpec(shape, lambda *_: (0,) * len(shape), pipeline_mode=pl.Buffered(1))


def _ffn_kernel(x_ref, g_ref, wg_ref, wu_ref, wd_ref, o_ref):
    x = x_ref[...]
    h = (_rms(x) * g_ref[...]).astype(BF16)
    gate = jnp.dot(h, wg_ref[...], preferred_element_type=F32)
    up = jnp.dot(h, wu_ref[...], preferred_element_type=F32)
    act = (gate * (1.0 / (1.0 + jnp.exp(-gate))) * up).astype(BF16)
    o_ref[...] = x + 0.5 * jnp.dot(act, wd_ref[...], preferred_element_type=F32)


def _half_swiglu(x, g, wg, wu, wd):
    n, d = x.shape
    f = wg.shape[1]
    row = pl.BlockSpec((ROW_TILE, d), lambda i: (i, 0))
    return pl.pallas_call(
        _ffn_kernel,
        grid=(n // ROW_TILE,),
        in_specs=[row, _resident((1, d)), _resident((d, f)), _resident((d, f)), _resident((f, d))],
        out_specs=row,
        out_shape=jax.ShapeDtypeStruct((n, d), F32),
        compiler_params=_params(1),
    )(x, g.reshape(1, d), wg, wu, wd)


_SEG = {"aq": (0, 512), "ak": (512, 512), "av": (1024, 512),
        "bq": (1536, 256), "bk": (1792, 256), "bv": (2048, 256),
        "cq": (2304, 256), "ck": (2560, 256), "cv": (2816, 256)}
_SEG_ORDER = ("aq", "ak", "av", "bq", "bk", "bv", "cq", "ck", "cv")
PROJ_COLS = 3072


def _norm_rope(seg, head_sum, gain, cos, sin):
    sq = seg * seg
    hi = sq.astype(BF16)
    lo = (sq - hi.astype(F32)).astype(BF16)
    ss = (jnp.dot(hi, head_sum, preferred_element_type=F32)
          + jnp.dot(lo, head_sum, preferred_element_type=F32))
    y = seg * lax.rsqrt(ss * (1.0 / HEAD_DIM) + EPS)
    out = []
    for k in range(2):
        yk = y[:, k * PAIR:(k + 1) * PAIR] * gain
        out.append(yk * cos + pltpu.roll(yk, HEAD_DIM, 1) * sin)
    return jnp.concatenate(out, axis=1)


def _proj_kernel(x_ref, g_ref, w_ref, hs_ref, gain_ref, cos1_ref, sin1_ref, cosx_ref, sinx_ref,
                 *refs):
    outs, stage_ref = dict(zip(_SEG_ORDER, refs[:-1])), refs[-1]
    h = (_rms(x_ref[...]) * g_ref[...]).astype(BF16)
    proj = jnp.dot(h, w_ref[...], preferred_element_type=F32)
    head_sum = hs_ref[...]
    tables = {"a": (cos1_ref[...], sin1_ref[...]), "b": (cosx_ref[...], sinx_ref[...]),
              "c": (cos1_ref[...], sin1_ref[...])}
    gain_row = {"aq": 0, "ak": 1, "bq": 2, "bk": 3, "cq": 4, "ck": 5}
    rows = proj.shape[0]
    for name in _SEG_ORDER:
        start, width = _SEG[name]
        out = outs[name]
        for c in range(width // 256):
            seg = proj[:, start + c * 256:start + (c + 1) * 256]
            if name in gain_row:
                gain = gain_ref[gain_row[name]:gain_row[name] + 1, :]
                seg = _norm_rope(seg, head_sum, gain, *tables[name[0]])
            if name[0] != "a":
                out[:, c * 256:(c + 1) * 256] = seg.astype(out.dtype)
                continue
            for k in range(2):
                slab = 2 * c + k
                stage_ref[slab] = seg[:, k * PAIR:(k + 1) * PAIR]
                for r in range(QUAD):
                    out[r, :, slab * PAIR:(slab + 1) * PAIR] = stage_ref[slab, pl.ds(r, rows // QUAD, stride=QUAD), :]


def _project(x, g, w, head_sum, gains, tables, batch, seq_len):
    n, d = x.shape
    blocks_per_seq = seq_len // ROW_TILE
    row = pl.BlockSpec((ROW_TILE, d), lambda i: (i, 0))
    table = pl.BlockSpec((ROW_TILE, PAIR), lambda i: (i % blocks_per_seq, 0))
    out_specs, out_shape = [], []
    for name in _SEG_ORDER:
        width = _SEG[name][1]
        if name[0] == "a":
            out_specs.append(pl.BlockSpec((None, QUAD, ROW_TILE // QUAD, width),
                                          lambda i: (i // blocks_per_seq, 0, i % blocks_per_seq, 0)))
            out_shape.append(jax.ShapeDtypeStruct((batch, QUAD, seq_len // QUAD, width), F32))
        else:
            out_specs.append(pl.BlockSpec((ROW_TILE, width), lambda i: (i, 0)))
            out_shape.append(jax.ShapeDtypeStruct((n, width), BF16))
    return pl.pallas_call(
        _proj_kernel,
        grid=(n // ROW_TILE,),
        in_specs=[row, _resident((1, d)), _resident((d, PROJ_COLS)), _resident((256, 256)),
                  _resident(gains.shape), table, table, table, table],
        out_specs=out_specs,
        out_shape=out_shape,
        scratch_shapes=[pltpu.VMEM((_SEG["aq"][1] // PAIR, ROW_TILE, PAIR), F32)],
        compiler_params=_params(1),
    )(x, g.reshape(1, d), w, head_sum, gains, *tables)


def _own_lanes(shape):
    lane = lax.broadcasted_iota(jnp.int32, shape, 1)
    return (lane & HALF) == 0


def _stack_heads(q):
    own = _own_lanes(q.shape)
    zero = jnp.zeros_like(q)
    return jnp.concatenate([jnp.where(own, q, zero), jnp.where(own, zero, q)], axis=0)


def _unstack_heads(x2):
    rows = x2.shape[0] // 2
    return jnp.where(_own_lanes((rows, PAIR)), x2[:rows], x2[rows:])


def _tile_attention(q, k, v, bias, sink=None):
    tq, window = q.shape[0], k.shape[0]
    s = lax.dot_general(_stack_heads(q.astype(BF16)), k.astype(BF16), (((1,), (1,)), ((), ())),
                        preferred_element_type=F32) + bias
    m2 = jnp.max(s, axis=1, keepdims=True)
    if sink is not None:
        m2 = jnp.maximum(m2, sink)
    p = jnp.exp(s - m2).astype(BF16)
    v1 = jnp.concatenate([v.astype(BF16), jnp.ones((window, PAIR), BF16)], axis=1)
    ul = jnp.dot(p, v1, preferred_element_type=F32)
    l2 = ul[:, PAIR:]
    if sink is not None:
        l2 = l2 + jnp.exp(sink - m2)
    return (_unstack_heads(ul[:, :PAIR]), _unstack_heads(jnp.broadcast_to(m2, (2 * tq, PAIR))),
            _unstack_heads(l2))


def _merge(u, m, l, u_old, m_old, l_old):
    m_new = jnp.maximum(m_old, m)
    w_old, w_new = jnp.exp(m_old - m_new), jnp.exp(m - m_new)
    return w_old * u_old + w_new * u, m_new, w_old * l_old + w_new * l


def _tile_window(i, n_tiles, radius):
    window = BAND_TILE + 2 * radius
    k_start = jnp.clip(i * BAND_TILE - radius, 0, n_tiles * BAND_TILE - window)
    return k_start, jnp.where(i == 0, 0, jnp.where(i == n_tiles - 1, 2, 1))


def _dilated_kernel(q_ref, k_ref, v_ref, bias_ref, bias1_ref, o_ref, u_acc, m_acc, l_acc):
    quarter = q_ref.shape[1]
    tq, radius = BAND_TILE, A_RADIUS
    window = tq + 2 * radius
    accs = (u_acc, m_acc, l_acc)

    n4 = quarter // tq

    def tile4(idx, carry):
        r, i = lax.div(idx, n4), lax.rem(idx, n4)
        k_start, variant = _tile_window(i, n4, radius)
        q_rows = pl.ds(pl.multiple_of(i * tq, tq), tq)
        k_rows = pl.ds(pl.multiple_of(k_start, radius), window)
        part = _tile_attention(q_ref[r, q_rows, :], k_ref[r, k_rows, :], v_ref[r, k_rows, :], bias_ref[variant])
        for acc, val in zip(accs, part):
            acc[r, q_rows, :] = val
        return carry

    lax.fori_loop(0, QUAD * n4, tile4, 0, unroll=BAND_UNROLL)

    n16 = quarter // QUAD // tq

    def tile16(idx, carry):
        res, i = lax.div(idx, n16), lax.rem(idx, n16)
        a, r = lax.div(res, QUAD), lax.rem(res, QUAD)
        k_start, variant = _tile_window(i, n16, radius)
        q_rows = pl.ds(i * tq * QUAD + a, tq, stride=QUAD)
        k_rows = pl.ds(k_start * QUAD + a, window, stride=QUAD)
        part = _tile_attention(q_ref[r, q_rows, :], k_ref[r, k_rows, :], v_ref[r, k_rows, :], bias_ref[variant])
        part = _merge(*part, *(acc[r, q_rows, :] for acc in accs))
        for acc, val in zip(accs, part):
            acc[r, q_rows, :] = val
        return carry

    lax.fori_loop(0, QUAD * QUAD * n16, tile16, 0, unroll=BAND_UNROLL)

    n1 = QUAD * quarter // tq

    def tile1(i, carry):
        k_start, variant = _tile_window(i, n1, radius)
        q_rows = pl.ds(pl.multiple_of(i * (tq // QUAD), tq // QUAD), tq // QUAD)
        k_rows = pl.ds(pl.multiple_of(k_start // QUAD, radius // QUAD), window // QUAD)
        gather = lambda ref, rows: jnp.concatenate([ref[r, rows, :] for r in range(QUAD)], axis=0)
        part = _tile_attention(gather(q_ref, q_rows), gather(k_ref, k_rows), gather(v_ref, k_rows),
                               bias1_ref[variant])
        u, _, l = _merge(*part, *(gather(acc, q_rows) for acc in accs))
        out = u / l
        for r in range(QUAD):
            o_ref[pl.ds(i * tq + r, tq // QUAD, stride=QUAD), :] = out[r * (tq // QUAD):(r + 1) * (tq // QUAD)]
        return carry

    lax.fori_loop(0, n1, tile1, 0, unroll=BAND_UNROLL)


def _dilated_attention(q, k, v):
    batch, _, quarter, width = q.shape
    seq_len = QUAD * quarter
    window = BAND_TILE + 2 * A_RADIUS
    assert DILATIONS == (1, QUAD, QUAD * QUAD) and quarter % (QUAD * BAND_TILE) == 0
    assert seq_len // (QUAD * QUAD) >= window
    bias = _window_bias(A_RADIUS)
    bias1 = _window_bias(A_RADIUS, interleave=QUAD)
    quad = pl.BlockSpec((None, QUAD, quarter, PAIR), lambda b, p: (b, 0, 0, p))
    return pl.pallas_call(
        _dilated_kernel,
        grid=(batch, width // PAIR),
        in_specs=[quad, quad, quad, _resident(bias.shape), _resident(bias1.shape)],
        out_specs=pl.BlockSpec((None, seq_len, PAIR), lambda b, p: (b, 0, p)),
        out_shape=jax.ShapeDtypeStruct((batch, seq_len, width), F32),
        scratch_shapes=[pltpu.VMEM((QUAD, quarter, PAIR), F32)] * 3,
        compiler_params=_params(2),
    )(q, k, v, bias, bias1)


def _sink_window_kernel(sink_ref, q_ref, k_ref, v_ref, bias_ref, o_ref):
    tq, radius = BAND_TILE, C_RADIUS
    window = tq + 2 * radius
    n_tiles = q_ref.shape[0] // tq
    pair = pl.program_id(1)
    first_head = lax.broadcasted_iota(jnp.int32, (2 * tq, 1), 0) < tq
    sink = jnp.where(first_head, sink_ref[2 * pair], sink_ref[2 * pair + 1])

    def tile(i, carry):
        k_start, variant = _tile_window(i, n_tiles, radius)
        q_rows = pl.ds(pl.multiple_of(i * tq, tq), tq)
        k_rows = pl.ds(pl.multiple_of(k_start, radius), window)
        u, _, l = _tile_attention(q_ref[q_rows, :], k_ref[k_rows, :], v_ref[k_rows, :], bias_ref[variant], sink)
        o_ref[q_rows, :] = u / l
        return carry

    lax.fori_loop(0, n_tiles, tile, 0, unroll=BAND_UNROLL)


def _sink_window_attention(q, k, v, sink):
    batch, seq_len, width = q.shape
    assert seq_len % BAND_TILE == 0 and seq_len >= BAND_TILE + 2 * C_RADIUS
    bias = _window_bias(C_RADIUS)
    seq = pl.BlockSpec((None, seq_len, PAIR), lambda b, p: (b, 0, p))
    return pl.pallas_call(
        _sink_window_kernel,
        grid=(batch, width // PAIR),
        in_specs=[pl.BlockSpec(memory_space=pltpu.SMEM), seq, seq, seq, _resident(bias.shape)],
        out_specs=seq,
        out_shape=jax.ShapeDtypeStruct(q.shape, F32),
        compiler_params=_params(2),
    )(sink.astype(F32), q, k, v, bias)


def _dense_kernel(q_ref, k_ref, v_ref, o_ref, v1_ref):
    seq_len = k_ref.shape[0]

    @pl.when(pl.program_id(2) == 0)
    def _():
        v1_ref[:, :PAIR] = v_ref[...]
        v1_ref[:, PAIR:] = jnp.ones((seq_len, PAIR), BF16)

    q2 = _stack_heads(q_ref[...])
    m = acc = None
    for start in range(0, seq_len, DENSE_KEYS):
        keys = slice(start, start + DENSE_KEYS)
        s = lax.dot_general(q2, k_ref[keys, :], (((1,), (1,)), ((), ())), preferred_element_type=F32)
        m_chunk = jnp.max(s, axis=1, keepdims=True)
        m_new = m_chunk if m is None else jnp.maximum(m, m_chunk)
        pv = jnp.dot(jnp.exp(s - m_new).astype(BF16), v1_ref[keys, :], preferred_element_type=F32)
        acc = pv if m is None else jnp.exp(m - m_new) * acc + pv
        m = m_new
    o_ref[...] = _unstack_heads(acc[:, :PAIR] / acc[:, PAIR:])


def _dense_attention(q, k, v):
    batch, seq_len, width = q.shape
    n_pairs = width // PAIR
    q_spec = pl.BlockSpec((None, DENSE_TILE, PAIR), lambda b, p, i: (b, i, p))
    kv_spec = pl.BlockSpec((None, seq_len, PAIR), lambda b, p, i: (b, 0, p))
    return pl.pallas_call(
        _dense_kernel,
        grid=(batch, n_pairs, seq_len // DENSE_TILE),
        in_specs=[q_spec, kv_spec, kv_spec],
        out_specs=q_spec,
        out_shape=jax.ShapeDtypeStruct(q.shape, F32),
        scratch_shapes=[pltpu.VMEM((seq_len, 2 * PAIR), BF16)],
        compiler_params=_params(3),
    )(q, k, v)


def _out_kernel(x_ref, oa_ref, ob_ref, oc_ref, g_ref, w_ref, y_ref):
    g = g_ref[...]
    wa, wb = oa_ref.shape[1], ob_ref.shape[1]
    mixed = jnp.concatenate([_rms(oa_ref[...]) * g[:, :wa],
                             _rms(ob_ref[...]) * g[:, wa:wa + wb],
                             _rms(oc_ref[...]) * g[:, wa + wb:]], axis=1)
    y_ref[...] = x_ref[...] + jnp.dot(mixed.astype(BF16), w_ref[...], preferred_element_type=F32)


def _mix_out(x, oa, ob, oc, g, w):
    n, d = x.shape
    rows = lambda width: pl.BlockSpec((ROW_TILE, width), lambda i: (i, 0))
    return pl.pallas_call(
        _out_kernel,
        grid=(n // ROW_TILE,),
        in_specs=[rows(d), rows(oa.shape[1]), rows(ob.shape[1]), rows(oc.shape[1]),
                  _resident((1, d)), _resident(w.shape)],
        out_specs=rows(d),
        out_shape=jax.ShapeDtypeStruct((n, d), F32),
        compiler_params=_params(1),
    )(x, oa, ob, oc, g.reshape(1, d), w)


def _mixer_columns():
    a = _pair_columns(A_HEADS // 2, _mha_pairs, False)
    b_q = _pair_columns(B_HEADS // 2, _mha_pairs, True)
    b_kv = _pair_columns(B_KV_HEADS, _dup_pairs, True)
    c_q = _pair_columns(C_HEADS // 2, _mha_pairs, False)
    c_kv = _pair_columns(C_KV_HEADS, _dup_pairs, False)
    sizes = [A_HEADS * 64] * 3 + [B_HEADS * 64, B_KV_HEADS * 64, B_KV_HEADS * 64,
                                  C_HEADS * 64, C_KV_HEADS * 64, C_KV_HEADS * 64]
    starts = np.concatenate([[0], np.cumsum(sizes)[:-1]])
    in_cols = np.concatenate([s + c for s, c in zip(starts, (a, a, a, b_q, b_kv, b_kv, c_q, c_kv, c_kv))])
    wa, wb = A_HEADS * 64, B_HEADS * 64
    out_rows = np.concatenate([a, wa + b_q, wa + wb + c_q])
    return in_cols, out_rows


def kernel(x, ffn1_norm, ffn1_w_gate, ffn1_w_up, ffn1_w_down, mix_norm, w_in, a_q_norm, a_k_norm, b_q_norm, b_k_norm, c_q_norm, c_k_norm, c_sink, group_norm, w_out, ffn2_norm, ffn2_w_gate, ffn2_w_up, ffn2_w_down):
    batch, seq_len, d = x.shape
    assert seq_len % ROW_TILE == 0 and seq_len % DENSE_TILE == 0 and seq_len % DENSE_KEYS == 0
    depth = w_in.shape[0]
    in_cols, out_rows = _mixer_columns()
    tables = _rope_lane_tables(seq_len)
    head_sum = _head_sum_matrix()
    d1, dx = _lane_dims(False), _lane_dims(True)
    q_scale = 1.0 / math.sqrt(HEAD_DIM)
    bf = lambda t: t.astype(BF16)
    shape3 = lambda t: t.reshape(batch, seq_len, t.shape[-1])
    flat = lambda t: t.reshape(batch * seq_len, t.shape[-1])

    xf = x.reshape(batch * seq_len, d)
    for l in range(depth):
        xf = _half_swiglu(xf, ffn1_norm[l], bf(ffn1_w_gate[l]), bf(ffn1_w_up[l]), bf(ffn1_w_down[l]))
        gains = jnp.stack([a_q_norm[l][d1] * q_scale, a_k_norm[l][d1],
                           b_q_norm[l][dx] * q_scale, b_k_norm[l][dx],
                           c_q_norm[l][d1] * q_scale, c_k_norm[l][d1]]).astype(F32)
        aq, ak, av, *rest = _project(xf, mix_norm[l], bf(w_in[l][:, in_cols]), head_sum, gains, tables,
                                     batch, seq_len)
        bq, bk, bv, cq, ck, cv = map(shape3, rest)
        oa = _dilated_attention(aq, ak, av)
        ob = _dense_attention(bq, bk, bv)
        oc = _sink_window_attention(cq, ck, cv, c_sink[l])
        xf = _mix_out(xf, flat(oa), flat(ob), flat(oc), group_norm[l][out_rows], bf(w_out[l][out_rows, :]))
        xf = _half_swiglu(xf, ffn2_norm[l], bf(ffn2_w_gate[l]), bf(ffn2_w_up[l]), bf(ffn2_w_down[l]))
    return xf.reshape(batch, seq_len, d)
```

```python
import functools
import math

import numpy as np
import jax
import jax.numpy as jnp
from jax import lax
from jax.experimental import pallas as pl
from jax.experimental.pallas import tpu as pltpu

HEAD_DIM = 64
HALF = HEAD_DIM // 2
PAIR = 2 * HEAD_DIM
A_HEADS, B_HEADS, B_KV_HEADS, C_HEADS, C_KV_HEADS = 8, 4, 2, 4, 2
DILATIONS = (1, 4, 16)
QUAD = 4
A_RADIUS = 64
C_RADIUS = 128
GRID_W = 64
ROPE_THETA = 10000.0
EPS = 1e-6
MASK_VALUE = -1e30
VMEM_LIMIT_BYTES = 56 * 1024 * 1024

ROW_TILE = 512
BAND_TILE = 128
BAND_UNROLL = 16
DENSE_TILE = 1024
DENSE_KEYS = 512

F32 = jnp.float32
BF16 = jnp.bfloat16


def _lane_dims(axial):
    lane = np.arange(PAIR)
    half, idx = lane // HEAD_DIM, lane % HALF
    if not axial:
        return half * HALF + idx
    quarter = HALF // 2
    return np.where(idx < quarter, half * quarter + idx, HALF + half * quarter + (idx - quarter))


def _lane_slot():
    lane = np.arange(PAIR)
    return (lane % HEAD_DIM) // HALF


def _pair_columns(n_pairs, heads_of_pair, axial):
    dims, slot = _lane_dims(axial), _lane_slot()
    cols = [np.asarray(heads_of_pair(p))[slot] * HEAD_DIM + dims for p in range(n_pairs)]
    return np.concatenate(cols)


def _mha_pairs(p):
    return (2 * p, 2 * p + 1)


def _dup_pairs(p):
    return (p, p)


def _rope_dim_tables(pos, dim):
    inv_freq = 1.0 / (ROPE_THETA ** (jnp.arange(0, dim, 2, dtype=F32) / dim))
    ang = pos.astype(F32)[:, None] * inv_freq[None, :]
    ang = jnp.concatenate([ang, ang], axis=-1)
    sign = jnp.concatenate([-jnp.ones((dim // 2,), F32), jnp.ones((dim // 2,), F32)])
    return jnp.cos(ang), jnp.sin(ang) * sign[None, :]


def _rope_lane_tables(seq_len):
    t = jnp.arange(seq_len)
    cos1, sin1 = _rope_dim_tables(t, HEAD_DIM)
    cos_r, sin_r = _rope_dim_tables(t // GRID_W, HALF)
    cos_c, sin_c = _rope_dim_tables(t % GRID_W, HALF)
    cos_x = jnp.concatenate([cos_r, cos_c], axis=-1)
    sin_x = jnp.concatenate([sin_r, sin_c], axis=-1)
    d1, dx = _lane_dims(False), _lane_dims(True)
    return cos1[:, d1], sin1[:, d1], cos_x[:, dx], sin_x[:, dx]


def _head_sum_matrix():
    slot = np.concatenate([_lane_slot(), 2 + _lane_slot()])
    return jnp.asarray(slot[:, None] == slot[None, :], dtype=BF16)


def _window_bias(radius, interleave=1):
    window = BAND_TILE + 2 * radius

    def offsets(count):
        index = np.arange(count)
        return (index % (count // interleave)) * interleave + index // (count // interleave)

    row = np.tile(offsets(BAND_TILE), 2)[:, None]
    col = offsets(window)[None, :]
    bias = [np.where(np.abs(col - off - row) <= radius, 0.0, MASK_VALUE) for off in (0, radius, 2 * radius)]
    return jnp.asarray(np.stack(bias), dtype=F32)


def _rms(x):
    return x * lax.rsqrt(jnp.mean(x * x, axis=-1, keepdims=True) + EPS)


def _params(n_axes):
    return pltpu.CompilerParams(dimension_semantics=("arbitrary",) * n_axes,
                                vmem_limit_bytes=VMEM_LIMIT_BYTES)


def _resident(shape):
    return pl.BlockSpec(shape, lambda *_: (0,) * len(shape), pipeline_mode=pl.Buffered(1))


def _swiglu_residual(x, g, wg_ref, wu_ref, wd_ref):
    h = (_rms(x) * g).astype(BF16)
    gate = jnp.dot(h, wg_ref[...], preferred_element_type=F32)
    up = jnp.dot(h, wu_ref[...], preferred_element_type=F32)
    act = (gate * (1.0 / (1.0 + jnp.exp(-gate))) * up).astype(BF16)
    return x + 0.5 * jnp.dot(act, wd_ref[...], preferred_element_type=F32)


def _ffn_kernel(x_ref, g_ref, wg_ref, wu_ref, wd_ref, o_ref):
    o_ref[...] = _swiglu_residual(x_ref[...], g_ref[...], wg_ref, wu_ref, wd_ref)


def _mix_ffn_kernel(x_ref, oa_ref, ob_ref, oc_ref, gg_ref, wo_ref, g_ref, wg_ref, wu_ref, wd_ref, o_ref):
    gg = gg_ref[...]
    wa, wb = oa_ref.shape[1], ob_ref.shape[1]
    mixed = jnp.concatenate([_rms(oa_ref[...]) * gg[:, :wa],
                             _rms(ob_ref[...]) * gg[:, wa:wa + wb],
                             _rms(oc_ref[...]) * gg[:, wa + wb:]], axis=1)
    x = x_ref[...] + jnp.dot(mixed.astype(BF16), wo_ref[...], preferred_element_type=F32)
    o_ref[...] = _swiglu_residual(x, g_ref[...], wg_ref, wu_ref, wd_ref)


def _half_swiglu(x, g, wg, wu, wd, mixers=None):
    n, d = x.shape
    f = wg.shape[1]
    rows = lambda width: pl.BlockSpec((ROW_TILE, width), lambda i: (i, 0))
    ffn_specs = [_resident((1, d)), _resident((d, f)), _resident((d, f)), _resident((f, d))]
    ffn_args = (g.reshape(1, d), wg, wu, wd)
    if mixers is None:
        body, in_specs, args = _ffn_kernel, [rows(d)] + ffn_specs, (x,) + ffn_args
    else:
        oa, ob, oc, gg, wo = mixers
        body = _mix_ffn_kernel
        in_specs = [rows(d), rows(oa.shape[1]), rows(ob.shape[1]), rows(oc.shape[1]),
                    _resident((1, d)), _resident(wo.shape)] + ffn_specs
        args = (x, oa, ob, oc, gg.reshape(1, d), wo) + ffn_args
    return pl.pallas_call(
        body,
        grid=(n // ROW_TILE,),
        in_specs=in_specs,
        out_specs=rows(d),
        out_shape=jax.ShapeDtypeStruct((n, d), F32),
        compiler_params=_params(1),
    )(*args)


_SEG = {"aq": (0, 512), "ak": (512, 512), "bq": (1024, 256), "bk": (1280, 256),
        "cq": (1536, 256), "ck": (1792, 256), "av": (2048, 512), "bv": (2560, 256), "cv": (2816, 256)}
_SEG_ORDER = tuple(_SEG)
PROJ_COLS = 3072


def _norm_rope(seg, head_sum, gain, cos, sin):
    ss = jnp.dot((seg * seg).astype(BF16), head_sum, preferred_element_type=F32)
    y = seg * lax.rsqrt(ss * (1.0 / HEAD_DIM) + EPS)
    out = []
    for k in range(2):
        yk = y[:, k * PAIR:(k + 1) * PAIR] * gain
        out.append(yk * cos + pltpu.roll(yk, HEAD_DIM, 1) * sin)
    return jnp.concatenate(out, axis=1)


def _proj_kernel(x_ref, g_ref, w_ref, hs_ref, gain_ref, cos1_ref, sin1_ref, cosx_ref, sinx_ref,
                 *refs):
    outs, stage_ref = dict(zip(_SEG_ORDER, refs[:-1])), refs[-1]
    h = (_rms(x_ref[...]) * g_ref[...]).astype(BF16)
    proj = jnp.dot(h, w_ref[...], preferred_element_type=F32)
    head_sum = hs_ref[...]
    tables = {"a": (cos1_ref[...], sin1_ref[...]), "b": (cosx_ref[...], sinx_ref[...]),
              "c": (cos1_ref[...], sin1_ref[...])}
    gain_row = {"aq": 0, "ak": 1, "bq": 2, "bk": 3, "cq": 4, "ck": 5}
    rows = h.shape[0]
    for name in _SEG_ORDER:
        start, width = _SEG[name]
        out = outs[name]
        for c in range(width // 256):
            seg = proj[:, start + c * 256:start + (c + 1) * 256]
            if name in gain_row:
                gain = gain_ref[gain_row[name]:gain_row[name] + 1, :]
                seg = _norm_rope(seg, head_sum, gain, *tables[name[0]])
            if name[0] != "a":
                out[:, c * 256:(c + 1) * 256] = seg.astype(out.dtype)
                continue
            for k in range(2):
                slab = 2 * c + k
                stage_ref[slab] = seg[:, k * PAIR:(k + 1) * PAIR]
                for r in range(QUAD):
                    out[r, :, slab * PAIR:(slab + 1) * PAIR] = stage_ref[slab, pl.ds(r, rows // QUAD, stride=QUAD), :]


def _project(x, g, w, head_sum, gains, tables, batch, seq_len):
    n, d = x.shape
    blocks_per_seq = seq_len // ROW_TILE
    row = pl.BlockSpec((ROW_TILE, d), lambda i: (i, 0))
    table = pl.BlockSpec((ROW_TILE, PAIR), lambda i: (i % blocks_per_seq, 0))
    out_specs, out_shape = [], []
    for name in _SEG_ORDER:
        width = _SEG[name][1]
        if name[0] == "a":
            out_specs.append(pl.BlockSpec((None, QUAD, ROW_TILE // QUAD, width),
                                          lambda i: (i // blocks_per_seq, 0, i % blocks_per_seq, 0)))
            out_shape.append(jax.ShapeDtypeStruct((batch, QUAD, seq_len // QUAD, width), F32))
        else:
            out_specs.append(pl.BlockSpec((ROW_TILE, width), lambda i: (i, 0)))
            out_shape.append(jax.ShapeDtypeStruct((n, width), BF16))
    return pl.pallas_call(
        _proj_kernel,
        grid=(n // ROW_TILE,),
        in_specs=[row, _resident((1, d)), _resident((d, PROJ_COLS)), _resident((256, 256)),
                  _resident(gains.shape), table, table, table, table],
        out_specs=out_specs,
        out_shape=out_shape,
        scratch_shapes=[pltpu.VMEM((_SEG["aq"][1] // PAIR, ROW_TILE, PAIR), F32)],
        compiler_params=_params(1),
    )(x, g.reshape(1, d), w, head_sum, gains, *tables)


def _own_lanes(shape):
    lane = lax.broadcasted_iota(jnp.int32, shape, 1)
    return (lane & HALF) == 0


def _stack_heads(q):
    own = _own_lanes(q.shape)
    zero = jnp.zeros_like(q)
    return jnp.concatenate([jnp.where(own, q, zero), jnp.where(own, zero, q)], axis=0)


def _unstack_heads(x2):
    rows = x2.shape[0] // 2
    return jnp.where(_own_lanes((rows, PAIR)), x2[:rows], x2[rows:])


def _tile_attention(q, k, v, bias, sink=None):
    tq, window = q.shape[0], k.shape[0]
    s = lax.dot_general(_stack_heads(q.astype(BF16)), k.astype(BF16), (((1,), (1,)), ((), ())),
                        preferred_element_type=F32) + bias
    m2 = jnp.max(s, axis=1, keepdims=True)
    if sink is not None:
        m2 = jnp.maximum(m2, sink)
    p = jnp.exp(s - m2).astype(BF16)
    v1 = jnp.concatenate([v.astype(BF16), jnp.ones((window, PAIR), BF16)], axis=1)
    ul = jnp.dot(p, v1, preferred_element_type=F32)
    l2 = ul[:, PAIR:]
    if sink is not None:
        l2 = l2 + jnp.exp(sink - m2)
    return (_unstack_heads(ul[:, :PAIR]), _unstack_heads(jnp.broadcast_to(m2, (2 * tq, PAIR))),
            _unstack_heads(l2))


def _merge(u, m, l, u_old, m_old, l_old):
    m_new = jnp.maximum(m_old, m)
    w_old, w_new = jnp.exp(m_old - m_new), jnp.exp(m - m_new)
    return w_old * u_old + w_new * u, m_new, w_old * l_old + w_new * l


def _tile_window(i, n_tiles, radius):
    window = BAND_TILE + 2 * radius
    k_start = jnp.clip(i * BAND_TILE - radius, 0, n_tiles * BAND_TILE - window)
    return k_start, jnp.where(i == 0, 0, jnp.where(i == n_tiles - 1, 2, 1))


def _dilated_kernel(q_ref, k_ref, v_ref, bias_ref, bias1_ref, o_ref, u_acc, m_acc, l_acc):
    quarter = q_ref.shape[1]
    tq, radius = BAND_TILE, A_RADIUS
    window = tq + 2 * radius
    accs = (u_acc, m_acc, l_acc)

    n4 = quarter // tq

    def tile4(idx, carry):
        r, i = lax.div(idx, n4), lax.rem(idx, n4)
        k_start, variant = _tile_window(i, n4, radius)
        q_rows = pl.ds(pl.multiple_of(i * tq, tq), tq)
        k_rows = pl.ds(pl.multiple_of(k_start, radius), window)
        part = _tile_attention(q_ref[r, q_rows, :], k_ref[r, k_rows, :], v_ref[r, k_rows, :], bias_ref[variant])
        for acc, val in zip(accs, part):
            acc[r, q_rows, :] = val
        return carry

    lax.fori_loop(0, QUAD * n4, tile4, 0, unroll=BAND_UNROLL)

    n16 = quarter // QUAD // tq

    def tile16(idx, carry):
        res, i = lax.div(idx, n16), lax.rem(idx, n16)
        a, r = lax.div(res, QUAD), lax.rem(res, QUAD)
        k_start, variant = _tile_window(i, n16, radius)
        q_rows = pl.ds(i * tq * QUAD + a, tq, stride=QUAD)
        k_rows = pl.ds(k_start * QUAD + a, window, stride=QUAD)
        part = _tile_attention(q_ref[r, q_rows, :], k_ref[r, k_rows, :], v_ref[r, k_rows, :], bias_ref[variant])
        part = _merge(*part, *(acc[r, q_rows, :] for acc in accs))
        for acc, val in zip(accs, part):
            acc[r, q_rows, :] = val
        return carry

    lax.fori_loop(0, QUAD * QUAD * n16, tile16, 0, unroll=BAND_UNROLL)

    n1 = QUAD * quarter // tq

    def tile1(i, carry):
        k_start, variant = _tile_window(i, n1, radius)
        q_rows = pl.ds(pl.multiple_of(i * (tq // QUAD), tq // QUAD), tq // QUAD)
        k_rows = pl.ds(pl.multiple_of(k_start // QUAD, radius // QUAD), window // QUAD)
        gather = lambda ref, rows: jnp.concatenate([ref[r, rows, :] for r in range(QUAD)], axis=0)
        part = _tile_attention(gather(q_ref, q_rows), gather(k_ref, k_rows), gather(v_ref, k_rows),
                               bias1_ref[variant])
        u, _, l = _merge(*part, *(gather(acc, q_rows) for acc in accs))
        out = u / l
        for r in range(QUAD):
            o_ref[pl.ds(i * tq + r, tq // QUAD, stride=QUAD), :] = out[r * (tq // QUAD):(r + 1) * (tq // QUAD)]
        return carry

    lax.fori_loop(0, n1, tile1, 0, unroll=BAND_UNROLL)


def _dilated_attention(q, k, v):
    batch, _, quarter, width = q.shape
    seq_len = QUAD * quarter
    window = BAND_TILE + 2 * A_RADIUS
    assert DILATIONS == (1, QUAD, QUAD * QUAD) and quarter % (QUAD * BAND_TILE) == 0
    assert seq_len // (QUAD * QUAD) >= window
    bias = _window_bias(A_RADIUS)
    bias1 = _window_bias(A_RADIUS, interleave=QUAD)
    quad = pl.BlockSpec((None, QUAD, quarter, PAIR), lambda b, p: (b, 0, 0, p))
    return pl.pallas_call(
        _dilated_kernel,
        grid=(batch, width // PAIR),
        in_specs=[quad, quad, quad, _resident(bias.shape), _resident(bias1.shape)],
        out_specs=pl.BlockSpec((None, seq_len, PAIR), lambda b, p: (b, 0, p)),
        out_shape=jax.ShapeDtypeStruct((batch, seq_len, width), F32),
        scratch_shapes=[pltpu.VMEM((QUAD, quarter, PAIR), F32)] * 3,
        compiler_params=_params(2),
    )(q, k, v, bias, bias1)


def _sink_window_kernel(sink_ref, q_ref, k_ref, v_ref, bias_ref, o_ref):
    tq, radius = BAND_TILE, C_RADIUS
    window = tq + 2 * radius
    n_tiles = q_ref.shape[0] // tq
    pair = pl.program_id(1)
    first_head = lax.broadcasted_iota(jnp.int32, (2 * tq, 1), 0) < tq
    sink = jnp.where(first_head, sink_ref[2 * pair], sink_ref[2 * pair + 1])

    def tile(i, carry):
        k_start, variant = _tile_window(i, n_tiles, radius)
        q_rows = pl.ds(pl.multiple_of(i * tq, tq), tq)
        k_rows = pl.ds(pl.multiple_of(k_start, radius), window)
        u, _, l = _tile_attention(q_ref[q_rows, :], k_ref[k_rows, :], v_ref[k_rows, :], bias_ref[variant], sink)
        o_ref[q_rows, :] = u / l
        return carry

    lax.fori_loop(0, n_tiles, tile, 0, unroll=BAND_UNROLL)


def _sink_window_attention(q, k, v, sink):
    batch, seq_len, width = q.shape
    assert seq_len % BAND_TILE == 0 and seq_len >= BAND_TILE + 2 * C_RADIUS
    bias = _window_bias(C_RADIUS)
    seq = pl.BlockSpec((None, seq_len, PAIR), lambda b, p: (b, 0, p))
    return pl.pallas_call(
        _sink_window_kernel,
        grid=(batch, width // PAIR),
        in_specs=[pl.BlockSpec(memory_space=pltpu.SMEM), seq, seq, seq, _resident(bias.shape)],
        out_specs=seq,
        out_shape=jax.ShapeDtypeStruct(q.shape, F32),
        compiler_params=_params(2),
    )(sink.astype(F32), q, k, v, bias)


def _dense_kernel(q_ref, k_ref, v_ref, o_ref, v1_ref):
    seq_len = k_ref.shape[0]

    @pl.when(pl.program_id(2) == 0)
    def _():
        v1_ref[:, :PAIR] = v_ref[...]
        v1_ref[:, PAIR:] = jnp.ones((seq_len, PAIR), BF16)

    q2 = _stack_heads(q_ref[...])
    m = acc = None
    for start in range(0, seq_len, DENSE_KEYS):
        keys = slice(start, start + DENSE_KEYS)
        s = lax.dot_general(q2, k_ref[keys, :], (((1,), (1,)), ((), ())), preferred_element_type=F32)
        m_chunk = jnp.max(s, axis=1, keepdims=True)
        m_new = m_chunk if m is None else jnp.maximum(m, m_chunk)
        pv = jnp.dot(jnp.exp(s - m_new).astype(BF16), v1_ref[keys, :], preferred_element_type=F32)
        acc = pv if m is None else jnp.exp(m - m_new) * acc + pv
        m = m_new
    o_ref[...] = _unstack_heads(acc[:, :PAIR] / acc[:, PAIR:])


def _dense_attention(q, k, v):
    batch, seq_len, width = q.shape
    n_pairs = width // PAIR
    q_spec = pl.BlockSpec((None, DENSE_TILE, PAIR), lambda b, p, i: (b, i, p))
    kv_spec = pl.BlockSpec((None, seq_len, PAIR), lambda b, p, i: (b, 0, p))
    return pl.pallas_call(
        _dense_kernel,
        grid=(batch, n_pairs, seq_len // DENSE_TILE),
        in_specs=[q_spec, kv_spec, kv_spec],
        out_specs=q_spec,
        out_shape=jax.ShapeDtypeStruct(q.shape, F32),
        scratch_shapes=[pltpu.VMEM((seq_len, 2 * PAIR), BF16)],
        compiler_params=_params(3),
    )(q, k, v)


def _mixer_columns():
    a = _pair_columns(A_HEADS // 2, _mha_pairs, False)
    b_q = _pair_columns(B_HEADS // 2, _mha_pairs, True)
    b_kv = _pair_columns(B_KV_HEADS, _dup_pairs, True)
    c_q = _pair_columns(C_HEADS // 2, _mha_pairs, False)
    c_kv = _pair_columns(C_KV_HEADS, _dup_pairs, False)
    names = ("aq", "ak", "av", "bq", "bk", "bv", "cq", "ck", "cv")
    sizes = [A_HEADS * 64] * 3 + [B_HEADS * 64, B_KV_HEADS * 64, B_KV_HEADS * 64,
                                  C_HEADS * 64, C_KV_HEADS * 64, C_KV_HEADS * 64]
    starts = np.concatenate([[0], np.cumsum(sizes)[:-1]])
    cols = {n: s + c for n, s, c in zip(names, starts, (a, a, a, b_q, b_kv, b_kv, c_q, c_kv, c_kv))}
    in_cols = np.concatenate([cols[n] for n in _SEG_ORDER])
    wa, wb = A_HEADS * 64, B_HEADS * 64
    out_rows = np.concatenate([a, wa + b_q, wa + wb + c_q])
    return in_cols, out_rows


def kernel(x, ffn1_norm, ffn1_w_gate, ffn1_w_up, ffn1_w_down, mix_norm, w_in, a_q_norm, a_k_norm, b_q_norm, b_k_norm, c_q_norm, c_k_norm, c_sink, group_norm, w_out, ffn2_norm, ffn2_w_gate, ffn2_w_up, ffn2_w_down):
    batch, seq_len, d = x.shape
    assert seq_len % ROW_TILE == 0 and seq_len % DENSE_TILE == 0 and seq_len % DENSE_KEYS == 0
    depth = w_in.shape[0]
    in_cols, out_rows = _mixer_columns()
    tables = _rope_lane_tables(seq_len)
    head_sum = _head_sum_matrix()
    d1, dx = _lane_dims(False), _lane_dims(True)
    q_scale = 1.0 / math.sqrt(HEAD_DIM)
    bf = lambda t: t.astype(BF16)
    shape3 = lambda t: t.reshape(batch, seq_len, t.shape[-1])
    flat = lambda t: t.reshape(batch * seq_len, t.shape[-1])

    xf = x.reshape(batch * seq_len, d)
    for l in range(depth):
        xf = _half_swiglu(xf, ffn1_norm[l], bf(ffn1_w_gate[l]), bf(ffn1_w_up[l]), bf(ffn1_w_down[l]))
        gains = jnp.stack([a_q_norm[l][d1] * q_scale, a_k_norm[l][d1],
                           b_q_norm[l][dx] * q_scale, b_k_norm[l][dx],
                           c_q_norm[l][d1] * q_scale, c_k_norm[l][d1]]).astype(F32)
        seg = dict(zip(_SEG_ORDER, _project(xf, mix_norm[l], bf(w_in[l][:, in_cols]), head_sum, gains, tables,
                                            batch, seq_len)))
        oa = _dilated_attention(seg["aq"], seg["ak"], seg["av"])
        ob = _dense_attention(shape3(seg["bq"]), shape3(seg["bk"]), shape3(seg["bv"]))
        oc = _sink_window_attention(shape3(seg["cq"]), shape3(seg["ck"]), shape3(seg["cv"]), c_sink[l])
        mixers = (flat(oa), flat(ob), flat(oc), group_norm[l][out_rows], bf(w_out[l][out_rows, :]))
        xf = _half_swiglu(xf, ffn2_norm[l], bf(ffn2_w_gate[l]), bf(ffn2_w_up[l]), bf(ffn2_w_down[l]), mixers)
    return xf.reshape(batch, seq_len, d)
```

```python
import math

import numpy as np
import jax
import jax.numpy as jnp
from jax import lax
from jax.experimental import pallas as pl
from jax.experimental.pallas import tpu as pltpu

HEAD_DIM = 64
HALF = HEAD_DIM // 2
PAIR = 2 * HEAD_DIM
A_HEADS, B_HEADS, B_KV_HEADS, C_HEADS, C_KV_HEADS = 8, 4, 2, 4, 2
DILATIONS = (1, 4, 16)
QUAD = 4
A_RADIUS = 64
C_RADIUS = 128
GRID_W = 64
ROPE_THETA = 10000.0
EPS = 1e-6
MASK_VALUE = -1e30
LOG2_E = math.log2(math.e)
VMEM_LIMIT_BYTES = 56 * 1024 * 1024

ROW_TILE = 512
BAND_TILE = 128
BAND_UNROLL = 16
DENSE_TILE = 1024
DENSE_KEYS = 512

F32 = jnp.float32
BF16 = jnp.bfloat16


def _lane_dims(axial):
    lane = np.arange(PAIR)
    half, idx = lane // HEAD_DIM, lane % HALF
    if not axial:
        return half * HALF + idx
    quarter = HALF // 2
    return np.where(idx < quarter, half * quarter + idx, HALF + half * quarter + (idx - quarter))


def _lane_slot():
    lane = np.arange(PAIR)
    return (lane % HEAD_DIM) // HALF


def _pair_columns(n_pairs, heads_of_pair, axial):
    dims, slot = _lane_dims(axial), _lane_slot()
    cols = [np.asarray(heads_of_pair(p))[slot] * HEAD_DIM + dims for p in range(n_pairs)]
    return np.concatenate(cols)


def _mha_pairs(p):
    return (2 * p, 2 * p + 1)


def _dup_pairs(p):
    return (p, p)


def _rope_dim_tables(pos, dim):
    inv_freq = 1.0 / (ROPE_THETA ** (jnp.arange(0, dim, 2, dtype=F32) / dim))
    ang = pos.astype(F32)[:, None] * inv_freq[None, :]
    ang = jnp.concatenate([ang, ang], axis=-1)
    sign = jnp.concatenate([-jnp.ones((dim // 2,), F32), jnp.ones((dim // 2,), F32)])
    return jnp.cos(ang), jnp.sin(ang) * sign[None, :]


def _rope_lane_tables(seq_len):
    t = jnp.arange(seq_len)
    cos1, sin1 = _rope_dim_tables(t, HEAD_DIM)
    cos_r, sin_r = _rope_dim_tables(t // GRID_W, HALF)
    cos_c, sin_c = _rope_dim_tables(t % GRID_W, HALF)
    cos_x = jnp.concatenate([cos_r, cos_c], axis=-1)
    sin_x = jnp.concatenate([sin_r, sin_c], axis=-1)
    d1, dx = _lane_dims(False), _lane_dims(True)
    return cos1[:, d1], sin1[:, d1], cos_x[:, dx], sin_x[:, dx]


def _head_sum_matrix():
    slot = np.concatenate([_lane_slot(), 2 + _lane_slot()])
    return jnp.asarray(slot[:, None] == slot[None, :], dtype=BF16)


def _window_bias(radius, interleave=1):
    window = BAND_TILE + 2 * radius

    def offsets(count):
        index = np.arange(count)
        return (index % (count // interleave)) * interleave + index // (count // interleave)

    row = np.tile(offsets(BAND_TILE), 2)[:, None]
    col = offsets(window)[None, :]
    bias = [np.where(np.abs(col - off - row) <= radius, 0.0, MASK_VALUE) for off in (0, radius, 2 * radius)]
    return jnp.asarray(np.stack(bias), dtype=F32)


def _rms(x):
    return x * lax.rsqrt(jnp.mean(x * x, axis=-1, keepdims=True) + EPS)


def _params(n_axes):
    return pltpu.CompilerParams(dimension_semantics=("arbitrary",) * n_axes,
                                vmem_limit_bytes=VMEM_LIMIT_BYTES)


def _resident(shape, layer=None):
    if layer is None:
        return pl.BlockSpec(shape, lambda *_: (0,) * len(shape), pipeline_mode=pl.Buffered(1))
    return pl.BlockSpec((None,) + tuple(shape), lambda *_: (layer,) + (0,) * len(shape),
                        pipeline_mode=pl.Buffered(1))


def _swiglu_residual(x, g, wg_ref, wu_ref, wd_ref):
    h = (_rms(x) * g).astype(BF16)
    gate = jnp.dot(h, wg_ref[...], preferred_element_type=F32)
    up = jnp.dot(h, wu_ref[...], preferred_element_type=F32)
    act = (gate * (1.0 / (1.0 + jnp.exp(-gate))) * up).astype(BF16)
    return x + 0.5 * jnp.dot(act, wd_ref[...], preferred_element_type=F32)


def _ffn_kernel(x_ref, g_ref, wg_ref, wu_ref, wd_ref, o_ref):
    o_ref[...] = _swiglu_residual(x_ref[...], g_ref[...], wg_ref, wu_ref, wd_ref)


def _mix_ffn_kernel(x_ref, oa_ref, ob_ref, oc_ref, gg_ref, wo_ref, g_ref, wg_ref, wu_ref, wd_ref, o_ref):
    gg = gg_ref[...]
    wa, wb = oa_ref.shape[1], ob_ref.shape[1]
    mixed = jnp.concatenate([_rms(oa_ref[...]) * gg[:, :wa],
                             _rms(ob_ref[...]) * gg[:, wa:wa + wb],
                             _rms(oc_ref[...]) * gg[:, wa + wb:]], axis=1)
    x = x_ref[...] + jnp.dot(mixed.astype(BF16), wo_ref[...], preferred_element_type=F32)
    o_ref[...] = _swiglu_residual(x, g_ref[...], wg_ref, wu_ref, wd_ref)


def _half_swiglu(x, layer, g, wg, wu, wd, mixers=None):
    n, d = x.shape
    f = wg.shape[-1]
    rows = lambda width: pl.BlockSpec((ROW_TILE, width), lambda i: (i, 0))
    ffn_specs = [_resident((1, d), layer), _resident((d, f), layer), _resident((d, f), layer),
                 _resident((f, d), layer)]
    if mixers is None:
        body, in_specs, args = _ffn_kernel, [rows(d)] + ffn_specs, (x, g, wg, wu, wd)
    else:
        oa, ob, oc, gg, wo = mixers
        body = _mix_ffn_kernel
        in_specs = [rows(d), rows(oa.shape[1]), rows(ob.shape[1]), rows(oc.shape[1]),
                    _resident((1, d), layer), _resident((d, d), layer)] + ffn_specs
        args = (x, oa, ob, oc, gg, wo, g, wg, wu, wd)
    return pl.pallas_call(
        body,
        grid=(n // ROW_TILE,),
        in_specs=in_specs,
        out_specs=rows(d),
        out_shape=jax.ShapeDtypeStruct((n, d), F32),
        compiler_params=_params(1),
    )(*args)


_SEG = {"aq": (0, 512), "ak": (512, 512), "bq": (1024, 256), "bk": (1280, 256),
        "cq": (1536, 256), "ck": (1792, 256), "av": (2048, 512), "bv": (2560, 256), "cv": (2816, 256)}
_SEG_ORDER = tuple(_SEG)
PROJ_COLS = 3072


def _norm_rope(seg, head_sum, gain, cos, sin):
    ss = jnp.dot((seg * seg).astype(BF16), head_sum, preferred_element_type=F32)
    y = seg * lax.rsqrt(ss * (1.0 / HEAD_DIM) + EPS)
    out = []
    for k in range(2):
        yk = y[:, k * PAIR:(k + 1) * PAIR] * gain
        out.append(yk * cos + pltpu.roll(yk, HEAD_DIM, 1) * sin)
    return jnp.concatenate(out, axis=1)


def _proj_kernel(x_ref, g_ref, w_ref, hs_ref, gain_ref, cos1_ref, sin1_ref, cosx_ref, sinx_ref,
                 *refs):
    outs, stage_ref = dict(zip(_SEG_ORDER, refs[:-1])), refs[-1]
    h = (_rms(x_ref[...]) * g_ref[...]).astype(BF16)
    proj = jnp.dot(h, w_ref[...], preferred_element_type=F32)
    head_sum = hs_ref[...]
    tables = {"a": (cos1_ref[...], sin1_ref[...]), "b": (cosx_ref[...], sinx_ref[...]),
              "c": (cos1_ref[...], sin1_ref[...])}
    gain_row = {"aq": 0, "ak": 1, "bq": 2, "bk": 3, "cq": 4, "ck": 5}
    rows = h.shape[0]
    for name in _SEG_ORDER:
        start, width = _SEG[name]
        out = outs[name]
        for c in range(width // 256):
            seg = proj[:, start + c * 256:start + (c + 1) * 256]
            if name in gain_row:
                gain = gain_ref[gain_row[name]:gain_row[name] + 1, :]
                seg = _norm_rope(seg, head_sum, gain, *tables[name[0]])
            if name[0] != "a":
                out[:, c * 256:(c + 1) * 256] = seg.astype(out.dtype)
                continue
            for k in range(2):
                slab = 2 * c + k
                stage_ref[slab] = seg[:, k * PAIR:(k + 1) * PAIR]
                for r in range(QUAD):
                    out[r, :, slab * PAIR:(slab + 1) * PAIR] = stage_ref[slab, pl.ds(r, rows // QUAD, stride=QUAD), :]


def _project(x, layer, g, w, head_sum, gains, tables, batch, seq_len):
    n, d = x.shape
    blocks_per_seq = seq_len // ROW_TILE
    row = pl.BlockSpec((ROW_TILE, d), lambda i: (i, 0))
    table = pl.BlockSpec((ROW_TILE, PAIR), lambda i: (i % blocks_per_seq, 0))
    out_specs, out_shape = [], []
    for name in _SEG_ORDER:
        width = _SEG[name][1]
        if name[0] == "a":
            out_specs.append(pl.BlockSpec((None, QUAD, ROW_TILE // QUAD, width),
                                          lambda i: (i // blocks_per_seq, 0, i % blocks_per_seq, 0)))
            out_shape.append(jax.ShapeDtypeStruct((batch, QUAD, seq_len // QUAD, width), F32))
        else:
            out_specs.append(pl.BlockSpec((ROW_TILE, width), lambda i: (i, 0)))
            out_shape.append(jax.ShapeDtypeStruct((n, width), BF16))
    return pl.pallas_call(
        _proj_kernel,
        grid=(n // ROW_TILE,),
        in_specs=[row, _resident((1, d), layer), _resident((d, PROJ_COLS), layer), _resident((256, 256)),
                  _resident(gains.shape[1:], layer), table, table, table, table],
        out_specs=out_specs,
        out_shape=out_shape,
        scratch_shapes=[pltpu.VMEM((_SEG["aq"][1] // PAIR, ROW_TILE, PAIR), F32)],
        compiler_params=_params(1),
    )(x, g, w, head_sum, gains, *tables)


def _own_lanes(shape):
    lane = lax.broadcasted_iota(jnp.int32, shape, 1)
    return (lane & HALF) == 0


def _stack_heads(q):
    own = _own_lanes(q.shape)
    zero = jnp.zeros_like(q)
    return jnp.concatenate([jnp.where(own, q, zero), jnp.where(own, zero, q)], axis=0)


def _unstack_heads(x2):
    rows = x2.shape[0] // 2
    return jnp.where(_own_lanes((rows, PAIR)), x2[:rows], x2[rows:])


def _tile_attention(q, k, v, bias, sink=None):
    tq, window = q.shape[0], k.shape[0]
    s = lax.dot_general(_stack_heads(q.astype(BF16)), k.astype(BF16), (((1,), (1,)), ((), ())),
                        preferred_element_type=F32) + bias
    m2 = jnp.max(s, axis=1, keepdims=True)
    if sink is not None:
        m2 = jnp.maximum(m2, sink)
    p = jnp.exp2(s - m2).astype(BF16)
    v1 = jnp.concatenate([v.astype(BF16), jnp.ones((window, PAIR), BF16)], axis=1)
    ul = jnp.dot(p, v1, preferred_element_type=F32)
    l2 = ul[:, PAIR:]
    if sink is not None:
        l2 = l2 + jnp.exp2(sink - m2)
    return (_unstack_heads(ul[:, :PAIR]), _unstack_heads(jnp.broadcast_to(m2, (2 * tq, PAIR))),
            _unstack_heads(l2))


def _merge(u, m, l, u_old, m_old, l_old):
    m_new = jnp.maximum(m_old, m)
    w_old, w_new = jnp.exp2(m_old - m_new), jnp.exp2(m - m_new)
    return w_old * u_old + w_new * u, m_new, w_old * l_old + w_new * l


def _tile_window(i, n_tiles, radius):
    window = BAND_TILE + 2 * radius
    k_start = jnp.clip(i * BAND_TILE - radius, 0, n_tiles * BAND_TILE - window)
    return k_start, jnp.where(i == 0, 0, jnp.where(i == n_tiles - 1, 2, 1))


def _dilated_kernel(q_ref, k_ref, v_ref, bias_ref, bias1_ref, o_ref, u_acc, m_acc, l_acc):
    quarter = q_ref.shape[1]
    tq, radius = BAND_TILE, A_RADIUS
    window = tq + 2 * radius
    accs = (u_acc, m_acc, l_acc)

    n4, n16, n1 = quarter // tq, quarter // QUAD // tq, QUAD * quarter // tq

    def tile4(r, i):
        k_start, variant = _tile_window(i, n4, radius)
        q_rows = pl.ds(pl.multiple_of(i * tq, tq), tq)
        k_rows = pl.ds(pl.multiple_of(k_start, radius), window)
        part = _tile_attention(q_ref[r, q_rows, :], k_ref[r, k_rows, :], v_ref[r, k_rows, :], bias_ref[variant])
        for acc, val in zip(accs, part):
            acc[r, q_rows, :] = val

    def tile16(a, r, i):
        k_start, variant = _tile_window(i, n16, radius)
        q_rows = pl.ds(i * tq * QUAD + a, tq, stride=QUAD)
        k_rows = pl.ds(k_start * QUAD + a, window, stride=QUAD)
        part = _tile_attention(q_ref[r, q_rows, :], k_ref[r, k_rows, :], v_ref[r, k_rows, :], bias_ref[variant])
        part = _merge(*part, *(acc[r, q_rows, :] for acc in accs))
        for acc, val in zip(accs, part):
            acc[r, q_rows, :] = val

    def tile1(i):
        k_start, variant = _tile_window(i, n1, radius)
        q_rows = pl.ds(pl.multiple_of(i * (tq // QUAD), tq // QUAD), tq // QUAD)
        k_rows = pl.ds(pl.multiple_of(k_start // QUAD, radius // QUAD), window // QUAD)
        gather = lambda ref, rows: jnp.concatenate([ref[r, rows, :] for r in range(QUAD)], axis=0)
        part = _tile_attention(gather(q_ref, q_rows), gather(k_ref, k_rows), gather(v_ref, k_rows),
                               bias1_ref[variant])
        u, _, l = _merge(*part, *(gather(acc, q_rows) for acc in accs))
        out = u / l
        for r in range(QUAD):
            o_ref[pl.ds(i * tq + r, tq // QUAD, stride=QUAD), :] = out[r * (tq // QUAD):(r + 1) * (tq // QUAD)]

    def span(s, carry):
        for r in range(QUAD):
            for j in range(QUAD):
                tile4(r, s * QUAD + j)
        for a in range(QUAD):
            for r in range(QUAD):
                tile16(a, r, s)
        for j in range(QUAD * QUAD):
            tile1(s * QUAD * QUAD + j)
        return carry

    lax.fori_loop(0, n16, span, 0)


def _dilated_attention(q, k, v):
    batch, _, quarter, width = q.shape
    seq_len = QUAD * quarter
    window = BAND_TILE + 2 * A_RADIUS
    assert DILATIONS == (1, QUAD, QUAD * QUAD) and quarter % (QUAD * BAND_TILE) == 0
    assert seq_len // (QUAD * QUAD) >= window
    bias = _window_bias(A_RADIUS)
    bias1 = _window_bias(A_RADIUS, interleave=QUAD)
    quad = pl.BlockSpec((None, QUAD, quarter, PAIR), lambda b, p: (b, 0, 0, p))
    return pl.pallas_call(
        _dilated_kernel,
        grid=(batch, width // PAIR),
        in_specs=[quad, quad, quad, _resident(bias.shape), _resident(bias1.shape)],
        out_specs=pl.BlockSpec((None, seq_len, PAIR), lambda b, p: (b, 0, p)),
        out_shape=jax.ShapeDtypeStruct((batch, seq_len, width), F32),
        scratch_shapes=[pltpu.VMEM((QUAD, quarter, PAIR), F32)] * 3,
        compiler_params=_params(2),
    )(q, k, v, bias, bias1)


def _sink_window_kernel(sink_ref, q_ref, k_ref, v_ref, bias_ref, o_ref):
    tq, radius = BAND_TILE, C_RADIUS
    window = tq + 2 * radius
    n_tiles = q_ref.shape[0] // tq
    pair = pl.program_id(1)
    first_head = lax.broadcasted_iota(jnp.int32, (2 * tq, 1), 0) < tq
    sink = jnp.where(first_head, sink_ref[2 * pair], sink_ref[2 * pair + 1])

    def tile(i, carry):
        k_start, variant = _tile_window(i, n_tiles, radius)
        q_rows = pl.ds(pl.multiple_of(i * tq, tq), tq)
        k_rows = pl.ds(pl.multiple_of(k_start, radius), window)
        u, _, l = _tile_attention(q_ref[q_rows, :], k_ref[k_rows, :], v_ref[k_rows, :], bias_ref[variant], sink)
        o_ref[q_rows, :] = u / l
        return carry

    lax.fori_loop(0, n_tiles, tile, 0, unroll=BAND_UNROLL)


def _sink_window_attention(q, k, v, sink):
    batch, seq_len, width = q.shape
    assert seq_len % BAND_TILE == 0 and seq_len >= BAND_TILE + 2 * C_RADIUS
    bias = _window_bias(C_RADIUS)
    seq = pl.BlockSpec((None, seq_len, PAIR), lambda b, p: (b, 0, p))
    return pl.pallas_call(
        _sink_window_kernel,
        grid=(batch, width // PAIR),
        in_specs=[pl.BlockSpec(memory_space=pltpu.SMEM), seq, seq, seq, _resident(bias.shape)],
        out_specs=seq,
        out_shape=jax.ShapeDtypeStruct(q.shape, F32),
        compiler_params=_params(2),
    )(sink, q, k, v, bias)


def _dense_kernel(q_ref, k_ref, v_ref, o_ref, v1_ref):
    seq_len = k_ref.shape[0]

    @pl.when(pl.program_id(2) == 0)
    def _():
        v1_ref[:, :PAIR] = v_ref[...]
        v1_ref[:, PAIR:] = jnp.ones((seq_len, PAIR), BF16)

    q2 = _stack_heads(q_ref[...])
    m = acc = None
    for start in range(0, seq_len, DENSE_KEYS):
        keys = slice(start, start + DENSE_KEYS)
        s = lax.dot_general(q2, k_ref[keys, :], (((1,), (1,)), ((), ())), preferred_element_type=F32)
        m_chunk = jnp.max(s, axis=1, keepdims=True)
        m_new = m_chunk if m is None else jnp.maximum(m, m_chunk)
        pv = jnp.dot(jnp.exp2(s - m_new).astype(BF16), v1_ref[keys, :], preferred_element_type=F32)
        acc = pv if m is None else jnp.exp2(m - m_new) * acc + pv
        m = m_new
    o_ref[...] = _unstack_heads(acc[:, :PAIR] / acc[:, PAIR:])


def _dense_attention(q, k, v):
    batch, seq_len, width = q.shape
    n_pairs = width // PAIR
    q_spec = pl.BlockSpec((None, DENSE_TILE, PAIR), lambda b, p, i: (b, i, p))
    kv_spec = pl.BlockSpec((None, seq_len, PAIR), lambda b, p, i: (b, 0, p))
    return pl.pallas_call(
        _dense_kernel,
        grid=(batch, n_pairs, seq_len // DENSE_TILE),
        in_specs=[q_spec, kv_spec, kv_spec],
        out_specs=q_spec,
        out_shape=jax.ShapeDtypeStruct(q.shape, F32),
        scratch_shapes=[pltpu.VMEM((seq_len, 2 * PAIR), BF16)],
        compiler_params=_params(3),
    )(q, k, v)


def _mixer_columns():
    a = _pair_columns(A_HEADS // 2, _mha_pairs, False)
    b_q = _pair_columns(B_HEADS // 2, _mha_pairs, True)
    b_kv = _pair_columns(B_KV_HEADS, _dup_pairs, True)
    c_q = _pair_columns(C_HEADS // 2, _mha_pairs, False)
    c_kv = _pair_columns(C_KV_HEADS, _dup_pairs, False)
    names = ("aq", "ak", "av", "bq", "bk", "bv", "cq", "ck", "cv")
    sizes = [A_HEADS * 64] * 3 + [B_HEADS * 64, B_KV_HEADS * 64, B_KV_HEADS * 64,
                                  C_HEADS * 64, C_KV_HEADS * 64, C_KV_HEADS * 64]
    starts = np.concatenate([[0], np.cumsum(sizes)[:-1]])
    cols = {n: s + c for n, s, c in zip(names, starts, (a, a, a, b_q, b_kv, b_kv, c_q, c_kv, c_kv))}
    in_cols = np.concatenate([cols[n] for n in _SEG_ORDER])
    wa, wb = A_HEADS * 64, B_HEADS * 64
    out_rows = np.concatenate([a, wa + b_q, wa + wb + c_q])
    return in_cols, out_rows


def kernel(x, ffn1_norm, ffn1_w_gate, ffn1_w_up, ffn1_w_down, mix_norm, w_in, a_q_norm, a_k_norm, b_q_norm, b_k_norm, c_q_norm, c_k_norm, c_sink, group_norm, w_out, ffn2_norm, ffn2_w_gate, ffn2_w_up, ffn2_w_down):
    batch, seq_len, d = x.shape
    assert seq_len % ROW_TILE == 0 and seq_len % DENSE_TILE == 0 and seq_len % DENSE_KEYS == 0
    depth = w_in.shape[0]
    in_cols, out_rows = _mixer_columns()
    tables = _rope_lane_tables(seq_len)
    head_sum = _head_sum_matrix()
    d1, dx = _lane_dims(False), _lane_dims(True)
    q_scale = LOG2_E / math.sqrt(HEAD_DIM)
    shape3 = lambda t: t.reshape(batch, seq_len, t.shape[-1])
    flat = lambda t: t.reshape(batch * seq_len, t.shape[-1])

    bf = lambda t: t.astype(BF16)
    row = lambda t: t.reshape(depth, 1, d)
    ffn1 = (row(ffn1_norm), bf(ffn1_w_gate), bf(ffn1_w_up), bf(ffn1_w_down))
    ffn2 = (row(ffn2_norm), bf(ffn2_w_gate), bf(ffn2_w_up), bf(ffn2_w_down))
    w_in_pairs, w_out_pairs = bf(w_in[:, :, in_cols]), bf(w_out[:, out_rows, :])
    group_gain = row(group_norm[:, out_rows])
    gains = jnp.stack([a_q_norm[:, d1] * q_scale, a_k_norm[:, d1], b_q_norm[:, dx] * q_scale, b_k_norm[:, dx],
                       c_q_norm[:, d1] * q_scale, c_k_norm[:, d1]], axis=1).astype(F32)
    sinks = c_sink.astype(F32) * LOG2_E

    xf = x.reshape(batch * seq_len, d)
    for l in range(depth):
        xf = _half_swiglu(xf, l, *ffn1)
        seg = dict(zip(_SEG_ORDER, _project(xf, l, row(mix_norm), w_in_pairs, head_sum, gains, tables,
                                            batch, seq_len)))
        oa = _dilated_attention(seg["aq"], seg["ak"], seg["av"])
        ob = _dense_attention(shape3(seg["bq"]), shape3(seg["bk"]), shape3(seg["bv"]))
        oc = _sink_window_attention(shape3(seg["cq"]), shape3(seg["ck"]), shape3(seg["cv"]), sinks[l])
        xf = _half_swiglu(xf, l, *ffn2, mixers=(flat(oa), flat(ob), flat(oc), group_gain, w_out_pairs))
    return xf.reshape(batch, seq_len, d)
```

```python
import math

import numpy as np
import jax
import jax.numpy as jnp
from jax import lax
from jax.experimental import pallas as pl
from jax.experimental.pallas import tpu as pltpu

HEAD_DIM = 64
HALF = HEAD_DIM // 2
PAIR = 2 * HEAD_DIM
A_HEADS, B_HEADS, B_KV_HEADS, C_HEADS, C_KV_HEADS = 8, 4, 2, 4, 2
DILATIONS = (1, 4, 16)
QUAD = 4
A_RADIUS = 64
C_RADIUS = 128
GRID_W = 64
ROPE_THETA = 10000.0
EPS = 1e-6
MASK_VALUE = -1e30
LOG2_E = math.log2(math.e)
VMEM_LIMIT_BYTES = 56 * 1024 * 1024

ROW_TILE = 512
BAND_TILE = 128
BAND_UNROLL = 16
DENSE_TILE = 1024
DENSE_KEYS = 256

F32 = jnp.float32
BF16 = jnp.bfloat16


def _lane_dims(axial):
    lane = np.arange(PAIR)
    half, idx = lane // HEAD_DIM, lane % HALF
    if not axial:
        return half * HALF + idx
    quarter = HALF // 2
    return np.where(idx < quarter, half * quarter + idx, HALF + half * quarter + (idx - quarter))


def _lane_slot():
    lane = np.arange(PAIR)
    return (lane % HEAD_DIM) // HALF


def _rope_dim_tables(pos, dim):
    inv_freq = 1.0 / (ROPE_THETA ** (jnp.arange(0, dim, 2, dtype=F32) / dim))
    ang = pos.astype(F32)[:, None] * inv_freq[None, :]
    ang = jnp.concatenate([ang, ang], axis=-1)
    sign = jnp.concatenate([-jnp.ones((dim // 2,), F32), jnp.ones((dim // 2,), F32)])
    return jnp.cos(ang), jnp.sin(ang) * sign[None, :]


def _rope_lane_tables(seq_len):
    t = jnp.arange(seq_len)
    cos1, sin1 = _rope_dim_tables(t, HEAD_DIM)
    cos_r, sin_r = _rope_dim_tables(t // GRID_W, HALF)
    cos_c, sin_c = _rope_dim_tables(t % GRID_W, HALF)
    cos_x = jnp.concatenate([cos_r, cos_c], axis=-1)
    sin_x = jnp.concatenate([sin_r, sin_c], axis=-1)
    d1, dx = _lane_dims(False), _lane_dims(True)
    return cos1[:, d1], sin1[:, d1], cos_x[:, dx], sin_x[:, dx]


def _head_sum_matrix():
    slot = np.concatenate([_lane_slot(), 2 + _lane_slot()])
    return jnp.asarray(slot[:, None] == slot[None, :], dtype=BF16)


def _window_bias(radius, interleave=1):
    window = BAND_TILE + 2 * radius

    def offsets(count):
        index = np.arange(count)
        return (index % (count // interleave)) * interleave + index // (count // interleave)

    row = np.tile(offsets(BAND_TILE), 2)[:, None]
    col = offsets(window)[None, :]
    bias = [np.where(np.abs(col - off - row) <= radius, 0.0, MASK_VALUE) for off in (0, radius, 2 * radius)]
    return jnp.asarray(np.stack(bias), dtype=F32)


def _rms(x):
    return x * lax.rsqrt(jnp.mean(x * x, axis=-1, keepdims=True) + EPS)


def _params(n_axes):
    return pltpu.CompilerParams(dimension_semantics=("arbitrary",) * n_axes,
                                vmem_limit_bytes=VMEM_LIMIT_BYTES)


def _resident(shape, layer=None):
    if layer is None:
        return pl.BlockSpec(shape, lambda *_: (0,) * len(shape), pipeline_mode=pl.Buffered(1))
    return pl.BlockSpec((None,) + tuple(shape), lambda *_: (layer,) + (0,) * len(shape),
                        pipeline_mode=pl.Buffered(1))


def _swiglu_residual(x, g, wg_ref, wu_ref, wd_ref):
    h = (_rms(x) * g).astype(BF16)
    gate = jnp.dot(h, wg_ref[...], preferred_element_type=F32)
    up = jnp.dot(h, wu_ref[...], preferred_element_type=F32)
    act = (gate * (1.0 / (1.0 + jnp.exp(-gate))) * up).astype(BF16)
    return x + 0.5 * jnp.dot(act, wd_ref[...], preferred_element_type=F32)


def _ffn_kernel(x_ref, g_ref, wg_ref, wu_ref, wd_ref, o_ref):
    o_ref[...] = _swiglu_residual(x_ref[...], g_ref[...], wg_ref, wu_ref, wd_ref)


def _mix_ffn_kernel(x_ref, oa_ref, ob_ref, oc_ref, gg_ref, wo_ref, g_ref, wg_ref, wu_ref, wd_ref, o_ref):
    gg = gg_ref[...]
    wa, wb = oa_ref.shape[1], ob_ref.shape[1]
    mixed = jnp.concatenate([_rms(oa_ref[...]) * gg[:, :wa],
                             _rms(ob_ref[...]) * gg[:, wa:wa + wb],
                             _rms(oc_ref[...]) * gg[:, wa + wb:]], axis=1)
    x = x_ref[...] + jnp.dot(mixed.astype(BF16), wo_ref[...], preferred_element_type=F32)
    o_ref[...] = _swiglu_residual(x, g_ref[...], wg_ref, wu_ref, wd_ref)


def _half_swiglu(x, layer, g, wg, wu, wd, mixers=None):
    n, d = x.shape
    f = wg.shape[-1]
    rows = lambda width: pl.BlockSpec((ROW_TILE, width), lambda i: (i, 0))
    ffn_specs = [_resident((1, d), layer), _resident((d, f), layer), _resident((d, f), layer),
                 _resident((f, d), layer)]
    if mixers is None:
        body, in_specs, args = _ffn_kernel, [rows(d)] + ffn_specs, (x, g, wg, wu, wd)
    else:
        oa, ob, oc, gg, wo = mixers
        body = _mix_ffn_kernel
        in_specs = [rows(d), rows(oa.shape[1]), rows(ob.shape[1]), rows(oc.shape[1]),
                    _resident((1, d), layer), _resident((d, d), layer)] + ffn_specs
        args = (x, oa, ob, oc, gg, wo, g, wg, wu, wd)
    return pl.pallas_call(
        body,
        grid=(n // ROW_TILE,),
        in_specs=in_specs,
        out_specs=rows(d),
        out_shape=jax.ShapeDtypeStruct((n, d), F32),
        compiler_params=_params(1),
    )(*args)


_SEG = {"av": (0, 512), "aq": (512, 512), "ak": (1024, 512), "bv": (1536, 256), "cv": (1792, 256),
        "bq": (2048, 256), "bk": (2304, 256), "cq": (2560, 256), "ck": (2816, 256)}
_SEG_ORDER = tuple(_SEG)
PROJ_COLS = 3072


def _norm_rope(seg, head_sum, gain, cos, sin):
    ss = jnp.dot((seg * seg).astype(BF16), head_sum, preferred_element_type=F32)
    y = seg * lax.rsqrt(ss * (1.0 / HEAD_DIM) + EPS)
    out = []
    for k in range(2):
        yk = y[:, k * PAIR:(k + 1) * PAIR] * gain
        out.append(yk * cos + pltpu.roll(yk, HEAD_DIM, 1) * sin)
    return jnp.concatenate(out, axis=1)


def _proj_kernel(x_ref, g_ref, w_ref, hs_ref, gain_ref, cos1_ref, sin1_ref, cosx_ref, sinx_ref,
                 *refs):
    outs, stage_ref = dict(zip(_SEG_ORDER, refs[:-1])), refs[-1]
    h = (_rms(x_ref[...]) * g_ref[...]).astype(BF16)
    proj = jnp.dot(h, w_ref[...], preferred_element_type=F32)
    head_sum = hs_ref[...]
    tables = {"a": (cos1_ref[...], sin1_ref[...]), "b": (cosx_ref[...], sinx_ref[...]),
              "c": (cos1_ref[...], sin1_ref[...])}
    gain_row = {"aq": 0, "ak": 1, "bq": 2, "bk": 3, "cq": 4, "ck": 5}
    rows = h.shape[0]
    for name in _SEG_ORDER:
        start, width = _SEG[name]
        out = outs[name]
        for c in range(width // 256):
            seg = proj[:, start + c * 256:start + (c + 1) * 256]
            if name in gain_row:
                gain = gain_ref[gain_row[name]:gain_row[name] + 1, :]
                seg = _norm_rope(seg, head_sum, gain, *tables[name[0]])
            if name[0] != "a":
                out[:, c * 256:(c + 1) * 256] = seg.astype(out.dtype)
                continue
            for k in range(2):
                slab = 2 * c + k
                stage_ref[slab] = seg[:, k * PAIR:(k + 1) * PAIR]
                for r in range(QUAD):
                    out[r, :, slab * PAIR:(slab + 1) * PAIR] = stage_ref[slab, pl.ds(r, rows // QUAD, stride=QUAD), :]


def _project(x, layer, g, w, head_sum, gains, tables, batch, seq_len):
    n, d = x.shape
    blocks_per_seq = seq_len // ROW_TILE
    row = pl.BlockSpec((ROW_TILE, d), lambda i: (i, 0))
    table = pl.BlockSpec((ROW_TILE, PAIR), lambda i: (i % blocks_per_seq, 0))
    out_specs, out_shape = [], []
    for name in _SEG_ORDER:
        width = _SEG[name][1]
        if name[0] == "a":
            out_specs.append(pl.BlockSpec((None, QUAD, ROW_TILE // QUAD, width),
                                          lambda i: (i // blocks_per_seq, 0, i % blocks_per_seq, 0)))
            out_shape.append(jax.ShapeDtypeStruct((batch, QUAD, seq_len // QUAD, width), F32))
        else:
            out_specs.append(pl.BlockSpec((ROW_TILE, width), lambda i: (i, 0)))
            out_shape.append(jax.ShapeDtypeStruct((n, width), BF16))
    return pl.pallas_call(
        _proj_kernel,
        grid=(n // ROW_TILE,),
        in_specs=[row, _resident((1, d), layer), _resident((d, PROJ_COLS), layer), _resident((256, 256)),
                  _resident(gains.shape[1:], layer), table, table, table, table],
        out_specs=out_specs,
        out_shape=out_shape,
        scratch_shapes=[pltpu.VMEM((_SEG["aq"][1] // PAIR, ROW_TILE, PAIR), F32)],
        compiler_params=_params(1),
    )(x, g, w, head_sum, gains, *tables)


def _own_lanes(shape):
    lane = lax.broadcasted_iota(jnp.int32, shape, 1)
    return (lane & HALF) == 0


def _stack_heads(q):
    own = _own_lanes(q.shape)
    zero = jnp.zeros_like(q)
    return jnp.concatenate([jnp.where(own, q, zero), jnp.where(own, zero, q)], axis=0)


def _unstack_heads(x2):
    rows = x2.shape[0] // 2
    return jnp.where(_own_lanes((rows, PAIR)), x2[:rows], x2[rows:])


def _tile_attention(q, k, v, bias, sink=None):
    tq, window = q.shape[0], k.shape[0]
    s = lax.dot_general(_stack_heads(q.astype(BF16)), k.astype(BF16), (((1,), (1,)), ((), ())),
                        preferred_element_type=F32) + bias
    m2 = jnp.max(s, axis=1, keepdims=True)
    if sink is not None:
        m2 = jnp.maximum(m2, sink)
    p = jnp.exp2(s - m2).astype(BF16)
    v1 = jnp.concatenate([v.astype(BF16), jnp.ones((window, PAIR), BF16)], axis=1)
    ul = jnp.dot(p, v1, preferred_element_type=F32)
    l2 = ul[:, PAIR:]
    if sink is not None:
        l2 = l2 + jnp.exp2(sink - m2)
    return (_unstack_heads(ul[:, :PAIR]), _unstack_heads(jnp.broadcast_to(m2, (2 * tq, PAIR))),
            _unstack_heads(l2))


def _merge(u, m, l, u_old, m_old, l_old):
    m_new = jnp.maximum(m_old, m)
    w_old, w_new = jnp.exp2(m_old - m_new), jnp.exp2(m - m_new)
    return w_old * u_old + w_new * u, m_new, w_old * l_old + w_new * l


def _tile_window(i, n_tiles, radius):
    window = BAND_TILE + 2 * radius
    k_start = jnp.clip(i * BAND_TILE - radius, 0, n_tiles * BAND_TILE - window)
    return k_start, jnp.where(i == 0, 0, jnp.where(i == n_tiles - 1, 2, 1))


def _dilated_kernel(q_ref, k_ref, v_ref, bias_ref, bias1_ref, o_ref, u_acc, m_acc, l_acc):
    quarter = q_ref.shape[1]
    tq, radius = BAND_TILE, A_RADIUS
    window = tq + 2 * radius
    accs = (u_acc, m_acc, l_acc)

    n4, n16, n1 = quarter // tq, quarter // QUAD // tq, QUAD * quarter // tq

    def tile4(r, i):
        k_start, variant = _tile_window(i, n4, radius)
        q_rows = pl.ds(pl.multiple_of(i * tq, tq), tq)
        k_rows = pl.ds(pl.multiple_of(k_start, radius), window)
        part = _tile_attention(q_ref[r, q_rows, :], k_ref[r, k_rows, :], v_ref[r, k_rows, :], bias_ref[variant])
        for acc, val in zip(accs, part):
            acc[r, q_rows, :] = val

    def tile16(a, r, i):
        k_start, variant = _tile_window(i, n16, radius)
        q_rows = pl.ds(i * tq * QUAD + a, tq, stride=QUAD)
        k_rows = pl.ds(k_start * QUAD + a, window, stride=QUAD)
        part = _tile_attention(q_ref[r, q_rows, :], k_ref[r, k_rows, :], v_ref[r, k_rows, :], bias_ref[variant])
        part = _merge(*part, *(acc[r, q_rows, :] for acc in accs))
        for acc, val in zip(accs, part):
            acc[r, q_rows, :] = val

    def tile1(i):
        k_start, variant = _tile_window(i, n1, radius)
        q_rows = pl.ds(pl.multiple_of(i * (tq // QUAD), tq // QUAD), tq // QUAD)
        k_rows = pl.ds(pl.multiple_of(k_start // QUAD, radius // QUAD), window // QUAD)
        gather = lambda ref, rows: jnp.concatenate([ref[r, rows, :] for r in range(QUAD)], axis=0)
        part = _tile_attention(gather(q_ref, q_rows), gather(k_ref, k_rows), gather(v_ref, k_rows),
                               bias1_ref[variant])
        u, _, l = _merge(*part, *(gather(acc, q_rows) for acc in accs))
        out = u / l
        for r in range(QUAD):
            o_ref[pl.ds(i * tq + r, tq // QUAD, stride=QUAD), :] = out[r * (tq // QUAD):(r + 1) * (tq // QUAD)]

    def span(s, carry):
        for r in range(QUAD):
            for j in range(QUAD):
                tile4(r, s * QUAD + j)
        for a in range(QUAD):
            for r in range(QUAD):
                tile16(a, r, s)
        for j in range(QUAD * QUAD):
            tile1(s * QUAD * QUAD + j)
        return carry

    lax.fori_loop(0, n16, span, 0)


def _dilated_attention(q, k, v):
    batch, _, quarter, width = q.shape
    seq_len = QUAD * quarter
    window = BAND_TILE + 2 * A_RADIUS
    assert DILATIONS == (1, QUAD, QUAD * QUAD) and quarter % (QUAD * BAND_TILE) == 0
    assert seq_len // (QUAD * QUAD) >= window
    bias = _window_bias(A_RADIUS)
    bias1 = _window_bias(A_RADIUS, interleave=QUAD)
    quad = pl.BlockSpec((None, QUAD, quarter, PAIR), lambda b, p: (b, 0, 0, p))
    return pl.pallas_call(
        _dilated_kernel,
        grid=(batch, width // PAIR),
        in_specs=[quad, quad, quad, _resident(bias.shape), _resident(bias1.shape)],
        out_specs=pl.BlockSpec((None, seq_len, PAIR), lambda b, p: (b, 0, p)),
        out_shape=jax.ShapeDtypeStruct((batch, seq_len, width), F32),
        scratch_shapes=[pltpu.VMEM((QUAD, quarter, PAIR), F32)] * 3,
        compiler_params=_params(2),
    )(q, k, v, bias, bias1)


def _sink_window_kernel(sink_ref, q_ref, k_ref, v_ref, bias_ref, o_ref):
    tq, radius = BAND_TILE, C_RADIUS
    window = tq + 2 * radius
    n_tiles = q_ref.shape[0] // tq
    pair = pl.program_id(1)
    first_head = lax.broadcasted_iota(jnp.int32, (2 * tq, 1), 0) < tq
    sink = jnp.where(first_head, sink_ref[2 * pair], sink_ref[2 * pair + 1])

    def tile(i, carry):
        k_start, variant = _tile_window(i, n_tiles, radius)
        q_rows = pl.ds(pl.multiple_of(i * tq, tq), tq)
        k_rows = pl.ds(pl.multiple_of(k_start, radius), window)
        u, _, l = _tile_attention(q_ref[q_rows, :], k_ref[k_rows, :], v_ref[k_rows, :], bias_ref[variant], sink)
        o_ref[q_rows, :] = u / l
        return carry

    lax.fori_loop(0, n_tiles, tile, 0, unroll=BAND_UNROLL)


def _sink_window_attention(q, k, v, sink):
    batch, seq_len, width = q.shape
    assert seq_len % BAND_TILE == 0 and seq_len >= BAND_TILE + 2 * C_RADIUS
    bias = _window_bias(C_RADIUS)
    seq = pl.BlockSpec((None, seq_len, PAIR), lambda b, p: (b, 0, p))
    return pl.pallas_call(
        _sink_window_kernel,
        grid=(batch, width // PAIR),
        in_specs=[pl.BlockSpec(memory_space=pltpu.SMEM), seq, seq, seq, _resident(bias.shape)],
        out_specs=seq,
        out_shape=jax.ShapeDtypeStruct(q.shape, F32),
        compiler_params=_params(2),
    )(sink, q, k, v, bias)


def _dense_kernel(q_ref, k_ref, v_ref, o_ref, v1_ref):
    seq_len = k_ref.shape[0]

    @pl.when(pl.program_id(2) == 0)
    def _():
        v1_ref[:, :PAIR] = v_ref[...]
        v1_ref[:, PAIR:] = jnp.ones((seq_len, PAIR), BF16)

    q2 = _stack_heads(q_ref[...])
    m = acc = None
    for start in range(0, seq_len, DENSE_KEYS):
        keys = slice(start, start + DENSE_KEYS)
        s = lax.dot_general(q2, k_ref[keys, :], (((1,), (1,)), ((), ())), preferred_element_type=F32)
        m_chunk = jnp.max(s, axis=1, keepdims=True)
        m_new = m_chunk if m is None else jnp.maximum(m, m_chunk)
        pv = jnp.dot(jnp.exp2(s - m_new).astype(BF16), v1_ref[keys, :], preferred_element_type=F32)
        acc = pv if m is None else jnp.exp2(m - m_new) * acc + pv
        m = m_new
    o_ref[...] = _unstack_heads(acc[:, :PAIR] / acc[:, PAIR:])


def _dense_attention(q, k, v):
    batch, seq_len, width = q.shape
    n_pairs = width // PAIR
    q_spec = pl.BlockSpec((None, DENSE_TILE, PAIR), lambda b, p, i: (b, i, p))
    kv_spec = pl.BlockSpec((None, seq_len, PAIR), lambda b, p, i: (b, 0, p))
    return pl.pallas_call(
        _dense_kernel,
        grid=(batch, n_pairs, seq_len // DENSE_TILE),
        in_specs=[q_spec, kv_spec, kv_spec],
        out_specs=q_spec,
        out_shape=jax.ShapeDtypeStruct(q.shape, F32),
        scratch_shapes=[pltpu.VMEM((seq_len, 2 * PAIR), BF16)],
        compiler_params=_params(3),
    )(q, k, v)


_IN_REGIONS = {"aq": (A_HEADS, False, False), "ak": (A_HEADS, False, False), "av": (A_HEADS, False, False),
               "bq": (B_HEADS, False, True), "bk": (B_KV_HEADS, True, True), "bv": (B_KV_HEADS, True, True),
               "cq": (C_HEADS, False, False), "ck": (C_KV_HEADS, True, False), "cv": (C_KV_HEADS, True, False)}
_OUT_REGIONS = {"oa": _IN_REGIONS["aq"], "ob": _IN_REGIONS["bq"], "oc": _IN_REGIONS["cq"]}


def _to_pair_layout(t, heads, dup, axial):
    lead = t.shape[:-1]
    if axial:
        t = t.reshape(lead + (heads, 2, 2, HALF // 2))
        t = jnp.swapaxes(t, -3, -2)
    t = t.reshape(lead + (heads, 2, HALF))
    if dup:
        t = jnp.broadcast_to(t[..., :, :, None, :], lead + (heads, 2, 2, HALF))
    else:
        t = jnp.swapaxes(t.reshape(lead + (heads // 2, 2, 2, HALF)), -3, -2)
    return t.reshape(lead + (-1,))


def _pair_regions(t, axis, regions, order):
    t = jnp.moveaxis(t, axis, -1)
    parts, start = {}, 0
    for name, (heads, dup, axial) in regions.items():
        parts[name] = _to_pair_layout(t[..., start:start + heads * HEAD_DIM], heads, dup, axial)
        start += heads * HEAD_DIM
    return jnp.moveaxis(jnp.concatenate([parts[name] for name in order], axis=-1), -1, axis)


def kernel(x, ffn1_norm, ffn1_w_gate, ffn1_w_up, ffn1_w_down, mix_norm, w_in, a_q_norm, a_k_norm, b_q_norm, b_k_norm, c_q_norm, c_k_norm, c_sink, group_norm, w_out, ffn2_norm, ffn2_w_gate, ffn2_w_up, ffn2_w_down):
    batch, seq_len, d = x.shape
    assert seq_len % ROW_TILE == 0 and seq_len % DENSE_TILE == 0 and seq_len % DENSE_KEYS == 0
    depth = w_in.shape[0]
    tables = _rope_lane_tables(seq_len)
    head_sum = _head_sum_matrix()
    d1, dx = _lane_dims(False), _lane_dims(True)
    q_scale = LOG2_E / math.sqrt(HEAD_DIM)
    shape3 = lambda t: t.reshape(batch, seq_len, t.shape[-1])
    flat = lambda t: t.reshape(batch * seq_len, t.shape[-1])

    bf = lambda t: t.astype(BF16)
    row = lambda t: t.reshape(depth, 1, d)
    ffn1 = (row(ffn1_norm), bf(ffn1_w_gate), bf(ffn1_w_up), bf(ffn1_w_down))
    ffn2 = (row(ffn2_norm), bf(ffn2_w_gate), bf(ffn2_w_up), bf(ffn2_w_down))
    w_in_pairs = _pair_regions(bf(w_in), 2, _IN_REGIONS, _SEG_ORDER)
    w_out_pairs = _pair_regions(bf(w_out), 1, _OUT_REGIONS, tuple(_OUT_REGIONS))
    group_gain = row(_pair_regions(group_norm, 1, _OUT_REGIONS, tuple(_OUT_REGIONS)))
    gains = jnp.stack([a_q_norm[:, d1] * q_scale, a_k_norm[:, d1], b_q_norm[:, dx] * q_scale, b_k_norm[:, dx],
                       c_q_norm[:, d1] * q_scale, c_k_norm[:, d1]], axis=1).astype(F32)
    sinks = c_sink.astype(F32) * LOG2_E

    xf = x.reshape(batch * seq_len, d)
    for l in range(depth):
        xf = _half_swiglu(xf, l, *ffn1)
        seg = dict(zip(_SEG_ORDER, _project(xf, l, row(mix_norm), w_in_pairs, head_sum, gains, tables,
                                            batch, seq_len)))
        oa = _dilated_attention(seg["aq"], seg["ak"], seg["av"])
        ob = _dense_attention(shape3(seg["bq"]), shape3(seg["bk"]), shape3(seg["bv"]))
        oc = _sink_window_attention(shape3(seg["cq"]), shape3(seg["ck"]), shape3(seg["cv"]), sinks[l])
        xf = _half_swiglu(xf, l, *ffn2, mixers=(flat(oa), flat(ob), flat(oc), group_gain, w_out_pairs))
    return xf.reshape(batch, seq_len, d)
```

```python
import math

import numpy as np
import jax
import jax.numpy as jnp
from jax import lax
from jax.experimental import pallas as pl
from jax.experimental.pallas import tpu as pltpu

HEAD_DIM = 64
HALF = HEAD_DIM // 2
PAIR = 2 * HEAD_DIM
A_HEADS, B_HEADS, B_KV_HEADS, C_HEADS, C_KV_HEADS = 8, 4, 2, 4, 2
DILATIONS = (1, 4, 16)
QUAD = 4
A_RADIUS = 64
C_RADIUS = 128
GRID_W = 64
ROPE_THETA = 10000.0
EPS = 1e-6
MASK_VALUE = -1e30
LOG2_E = math.log2(math.e)
VMEM_LIMIT_BYTES = 56 * 1024 * 1024

ROW_TILE = 512
PROJ_TILE = 1024
BAND_TILE = 128
BAND_UNROLL = 16
DENSE_TILE = 1024
DENSE_KEYS = 256

F32 = jnp.float32
BF16 = jnp.bfloat16


def _lane_dims(axial):
    lane = np.arange(PAIR)
    half, idx = lane // HEAD_DIM, lane % HALF
    if not axial:
        return half * HALF + idx
    quarter = HALF // 2
    return np.where(idx < quarter, half * quarter + idx, HALF + half * quarter + (idx - quarter))


def _lane_slot():
    lane = np.arange(PAIR)
    return (lane % HEAD_DIM) // HALF


def _rope_dim_tables(pos, dim):
    inv_freq = 1.0 / (ROPE_THETA ** (jnp.arange(0, dim, 2, dtype=F32) / dim))
    ang = pos.astype(F32)[:, None] * inv_freq[None, :]
    ang = jnp.concatenate([ang, ang], axis=-1)
    sign = jnp.concatenate([-jnp.ones((dim // 2,), F32), jnp.ones((dim // 2,), F32)])
    return jnp.cos(ang), jnp.sin(ang) * sign[None, :]


def _rope_lane_tables(seq_len):
    t = jnp.arange(seq_len)
    cos1, sin1 = _rope_dim_tables(t, HEAD_DIM)
    cos_r, sin_r = _rope_dim_tables(t // GRID_W, HALF)
    cos_c, sin_c = _rope_dim_tables(t % GRID_W, HALF)
    cos_x = jnp.concatenate([cos_r, cos_c], axis=-1)
    sin_x = jnp.concatenate([sin_r, sin_c], axis=-1)
    d1, dx = _lane_dims(False), _lane_dims(True)
    return cos1[:, d1], sin1[:, d1], cos_x[:, dx], sin_x[:, dx]


def _head_mean_matrix():
    slot = np.concatenate([_lane_slot(), 2 + _lane_slot()])
    return jnp.asarray((slot[:, None] == slot[None, :]) / HEAD_DIM, dtype=BF16)


def _window_bias(radius, interleave=1):
    window = BAND_TILE + 2 * radius

    def offsets(count):
        index = np.arange(count)
        return (index % (count // interleave)) * interleave + index // (count // interleave)

    row = np.tile(offsets(BAND_TILE), 2)[:, None]
    col = offsets(window)[None, :]
    bias = [np.where(np.abs(col - off - row) <= radius, 0.0, MASK_VALUE) for off in (0, radius, 2 * radius)]
    return jnp.asarray(np.stack(bias), dtype=F32)


def _rms(x):
    return x * lax.rsqrt(jnp.mean(x * x, axis=-1, keepdims=True) + EPS)


def _params(n_axes):
    return pltpu.CompilerParams(dimension_semantics=("arbitrary",) * n_axes,
                                vmem_limit_bytes=VMEM_LIMIT_BYTES)


def _resident(shape, layer=None):
    if layer is None:
        return pl.BlockSpec(shape, lambda *_: (0,) * len(shape), pipeline_mode=pl.Buffered(1))
    return pl.BlockSpec((None,) + tuple(shape), lambda *_: (layer,) + (0,) * len(shape),
                        pipeline_mode=pl.Buffered(1))


def _swiglu_residual(x, g, wg_ref, wu_ref, wd_ref):
    h = (_rms(x) * g).astype(BF16)
    gate = jnp.dot(h, wg_ref[...], preferred_element_type=F32)
    up = jnp.dot(h, wu_ref[...], preferred_element_type=F32)
    act = (gate * (1.0 / (1.0 + jnp.exp(-gate))) * up).astype(BF16)
    return x + 0.5 * jnp.dot(act, wd_ref[...], preferred_element_type=F32)


def _ffn_kernel(x_ref, g_ref, wg_ref, wu_ref, wd_ref, o_ref):
    o_ref[...] = _swiglu_residual(x_ref[...], g_ref[...], wg_ref, wu_ref, wd_ref)


def _mix_ffn_kernel(x_ref, oa_ref, ob_ref, oc_ref, gg_ref, wo_ref, g_ref, wg_ref, wu_ref, wd_ref, o_ref):
    gg = gg_ref[...]
    wa, wb = oa_ref.shape[1], ob_ref.shape[1]
    mixed = jnp.concatenate([_rms(oa_ref[...]) * gg[:, :wa],
                             _rms(ob_ref[...]) * gg[:, wa:wa + wb],
                             _rms(oc_ref[...]) * gg[:, wa + wb:]], axis=1)
    x = x_ref[...] + jnp.dot(mixed.astype(BF16), wo_ref[...], preferred_element_type=F32)
    o_ref[...] = _swiglu_residual(x, g_ref[...], wg_ref, wu_ref, wd_ref)


def _half_swiglu(x, layer, g, wg, wu, wd, mixers=None):
    n, d = x.shape
    f = wg.shape[-1]
    rows = lambda width: pl.BlockSpec((ROW_TILE, width), lambda i: (i, 0))
    ffn_specs = [_resident((1, d), layer), _resident((d, f), layer), _resident((d, f), layer),
                 _resident((f, d), layer)]
    if mixers is None:
        body, in_specs, args = _ffn_kernel, [rows(d)] + ffn_specs, (x, g, wg, wu, wd)
    else:
        oa, ob, oc, gg, wo = mixers
        body = _mix_ffn_kernel
        in_specs = [rows(d), rows(oa.shape[1]), rows(ob.shape[1]), rows(oc.shape[1]),
                    _resident((1, d), layer), _resident((d, d), layer)] + ffn_specs
        args = (x, oa, ob, oc, gg, wo, g, wg, wu, wd)
    return pl.pallas_call(
        body,
        grid=(n // ROW_TILE,),
        in_specs=in_specs,
        out_specs=rows(d),
        out_shape=jax.ShapeDtypeStruct((n, d), F32),
        compiler_params=_params(1),
    )(*args)


_SEG = {"av": (0, 512), "aq": (512, 512), "ak": (1024, 512), "bv": (1536, 256), "cv": (1792, 256),
        "bq": (2048, 256), "bk": (2304, 256), "cq": (2560, 256), "ck": (2816, 256)}
_SEG_ORDER = tuple(_SEG)
PROJ_COLS = 3072


def _norm_rope(seg, head_mean, gain, cos, sin):
    mean_sq = jnp.dot((seg * seg).astype(BF16), head_mean, preferred_element_type=F32)
    y = seg * lax.rsqrt(mean_sq + EPS)
    out = []
    for k in range(2):
        yk = y[:, k * PAIR:(k + 1) * PAIR] * gain
        out.append(yk * cos + pltpu.roll(yk, HEAD_DIM, 1) * sin)
    return jnp.concatenate(out, axis=1)


def _proj_kernel(x_ref, g_ref, w_ref, hs_ref, gain_ref, cos1_ref, sin1_ref, cosx_ref, sinx_ref,
                 *refs):
    outs, stage_ref = dict(zip(_SEG_ORDER, refs[:-1])), refs[-1]
    h = (_rms(x_ref[...]) * g_ref[...]).astype(BF16)
    proj = jnp.dot(h, w_ref[...], preferred_element_type=F32)
    head_mean = hs_ref[...]
    tables = {"a": (cos1_ref[...], sin1_ref[...]), "b": (cosx_ref[...], sinx_ref[...]),
              "c": (cos1_ref[...], sin1_ref[...])}
    gain_row = {"aq": 0, "ak": 1, "bq": 2, "bk": 3, "cq": 4, "ck": 5}
    rows = h.shape[0]
    for name in _SEG_ORDER:
        start, width = _SEG[name]
        out = outs[name]
        for c in range(width // 256):
            seg = proj[:, start + c * 256:start + (c + 1) * 256]
            if name in gain_row:
                gain = gain_ref[gain_row[name]:gain_row[name] + 1, :]
                seg = _norm_rope(seg, head_mean, gain, *tables[name[0]])
            if name[0] != "a":
                out[:, c * 256:(c + 1) * 256] = seg.astype(out.dtype)
                continue
            for k in range(2):
                slab = 2 * c + k
                stage_ref[slab] = seg[:, k * PAIR:(k + 1) * PAIR]
                for r in range(QUAD):
                    out[r, :, slab * PAIR:(slab + 1) * PAIR] = stage_ref[slab, pl.ds(r, rows // QUAD, stride=QUAD), :]


def _project(x, layer, g, w, head_mean, gains, tables, batch, seq_len):
    n, d = x.shape
    blocks_per_seq = seq_len // PROJ_TILE
    row = pl.BlockSpec((PROJ_TILE, d), lambda i: (i, 0))
    table = pl.BlockSpec((PROJ_TILE, PAIR), lambda i: (i % blocks_per_seq, 0))
    out_specs, out_shape = [], []
    for name in _SEG_ORDER:
        width = _SEG[name][1]
        if name[0] == "a":
            out_specs.append(pl.BlockSpec((None, QUAD, PROJ_TILE // QUAD, width),
                                          lambda i: (i // blocks_per_seq, 0, i % blocks_per_seq, 0)))
            out_shape.append(jax.ShapeDtypeStruct((batch, QUAD, seq_len // QUAD, width), F32))
        else:
            out_specs.append(pl.BlockSpec((PROJ_TILE, width), lambda i: (i, 0)))
            out_shape.append(jax.ShapeDtypeStruct((n, width), BF16))
    return pl.pallas_call(
        _proj_kernel,
        grid=(n // PROJ_TILE,),
        in_specs=[row, _resident((1, d), layer), _resident((d, PROJ_COLS), layer), _resident((256, 256)),
                  _resident(gains.shape[1:], layer), table, table, table, table],
        out_specs=out_specs,
        out_shape=out_shape,
        scratch_shapes=[pltpu.VMEM((_SEG["aq"][1] // PAIR, PROJ_TILE, PAIR), F32)],
        compiler_params=_params(1),
    )(x, g, w, head_mean, gains, *tables)


def _own_lanes(shape):
    lane = lax.broadcasted_iota(jnp.int32, shape, 1)
    return (lane & HALF) == 0


def _stack_heads(q):
    own = _own_lanes(q.shape)
    zero = jnp.zeros_like(q)
    return jnp.concatenate([jnp.where(own, q, zero), jnp.where(own, zero, q)], axis=0)


def _unstack_heads(x2):
    rows = x2.shape[0] // 2
    return jnp.where(_own_lanes((rows, PAIR)), x2[:rows], x2[rows:])


def _tile_attention(q, k, v, bias, sink=None):
    tq, window = q.shape[0], k.shape[0]
    s = lax.dot_general(_stack_heads(q.astype(BF16)), k.astype(BF16), (((1,), (1,)), ((), ())),
                        preferred_element_type=F32) + bias
    m2 = jnp.max(s, axis=1, keepdims=True)
    if sink is not None:
        m2 = jnp.maximum(m2, sink)
    p = jnp.exp2(s - m2).astype(BF16)
    v1 = jnp.concatenate([v.astype(BF16), jnp.ones((window, PAIR), BF16)], axis=1)
    ul = jnp.dot(p, v1, preferred_element_type=F32)
    l2 = ul[:, PAIR:]
    if sink is not None:
        l2 = l2 + jnp.exp2(sink - m2)
    return (_unstack_heads(ul[:, :PAIR]), _unstack_heads(jnp.broadcast_to(m2, (2 * tq, PAIR))),
            _unstack_heads(l2))


def _merge(u, m, l, u_old, m_old, l_old):
    m_new = jnp.maximum(m_old, m)
    w_old, w_new = jnp.exp2(m_old - m_new), jnp.exp2(m - m_new)
    return w_old * u_old + w_new * u, m_new, w_old * l_old + w_new * l


def _tile_window(i, n_tiles, radius):
    window = BAND_TILE + 2 * radius
    k_start = jnp.clip(i * BAND_TILE - radius, 0, n_tiles * BAND_TILE - window)
    return k_start, jnp.where(i == 0, 0, jnp.where(i == n_tiles - 1, 2, 1))


def _dilated_kernel(q_ref, k_ref, v_ref, bias_ref, bias1_ref, o_ref, u_acc, m_acc, l_acc):
    quarter = q_ref.shape[1]
    tq, radius = BAND_TILE, A_RADIUS
    window = tq + 2 * radius
    accs = (u_acc, m_acc, l_acc)

    n4, n16, n1 = quarter // tq, quarter // QUAD // tq, QUAD * quarter // tq

    def tile4(r, i):
        k_start, variant = _tile_window(i, n4, radius)
        q_rows = pl.ds(pl.multiple_of(i * tq, tq), tq)
        k_rows = pl.ds(pl.multiple_of(k_start, radius), window)
        part = _tile_attention(q_ref[r, q_rows, :], k_ref[r, k_rows, :], v_ref[r, k_rows, :], bias_ref[variant])
        for acc, val in zip(accs, part):
            acc[r, q_rows, :] = val

    def tile16(a, r, i):
        k_start, variant = _tile_window(i, n16, radius)
        q_rows = pl.ds(i * tq * QUAD + a, tq, stride=QUAD)
        k_rows = pl.ds(k_start * QUAD + a, window, stride=QUAD)
        part = _tile_attention(q_ref[r, q_rows, :], k_ref[r, k_rows, :], v_ref[r, k_rows, :], bias_ref[variant])
        part = _merge(*part, *(acc[r, q_rows, :] for acc in accs))
        for acc, val in zip(accs, part):
            acc[r, q_rows, :] = val

    def tile1(i):
        k_start, variant = _tile_window(i, n1, radius)
        q_rows = pl.ds(pl.multiple_of(i * (tq // QUAD), tq // QUAD), tq // QUAD)
        k_rows = pl.ds(pl.multiple_of(k_start // QUAD, radius // QUAD), window // QUAD)
        gather = lambda ref, rows: jnp.concatenate([ref[r, rows, :] for r in range(QUAD)], axis=0)
        part = _tile_attention(gather(q_ref, q_rows), gather(k_ref, k_rows), gather(v_ref, k_rows),
                               bias1_ref[variant])
        u, _, l = _merge(*part, *(gather(acc, q_rows) for acc in accs))
        out = u / l
        for r in range(QUAD):
            o_ref[pl.ds(i * tq + r, tq // QUAD, stride=QUAD), :] = out[r * (tq // QUAD):(r + 1) * (tq // QUAD)]

    def span(s, carry):
        for r in range(QUAD):
            for j in range(QUAD):
                tile4(r, s * QUAD + j)
        for a in range(QUAD):
            for r in range(QUAD):
                tile16(a, r, s)
        for j in range(QUAD * QUAD):
            tile1(s * QUAD * QUAD + j)
        return carry

    lax.fori_loop(0, n16, span, 0)


def _dilated_attention(q, k, v):
    batch, _, quarter, width = q.shape
    seq_len = QUAD * quarter
    window = BAND_TILE + 2 * A_RADIUS
    assert DILATIONS == (1, QUAD, QUAD * QUAD) and quarter % (QUAD * BAND_TILE) == 0
    assert seq_len // (QUAD * QUAD) >= window
    bias = _window_bias(A_RADIUS)
    bias1 = _window_bias(A_RADIUS, interleave=QUAD)
    quad = pl.BlockSpec((None, QUAD, quarter, PAIR), lambda b, p: (b, 0, 0, p))
    return pl.pallas_call(
        _dilated_kernel,
        grid=(batch, width // PAIR),
        in_specs=[quad, quad, quad, _resident(bias.shape), _resident(bias1.shape)],
        out_specs=pl.BlockSpec((None, seq_len, PAIR), lambda b, p: (b, 0, p)),
        out_shape=jax.ShapeDtypeStruct((batch, seq_len, width), F32),
        scratch_shapes=[pltpu.VMEM((QUAD, quarter, PAIR), F32)] * 3,
        compiler_params=_params(2),
    )(q, k, v, bias, bias1)


def _sink_window_kernel(sink_ref, q_ref, k_ref, v_ref, bias_ref, o_ref):
    tq, radius = BAND_TILE, C_RADIUS
    window = tq + 2 * radius
    n_tiles = q_ref.shape[0] // tq
    pair = pl.program_id(1)
    first_head = lax.broadcasted_iota(jnp.int32, (2 * tq, 1), 0) < tq
    sink = jnp.where(first_head, sink_ref[2 * pair], sink_ref[2 * pair + 1])

    def tile(i, carry):
        k_start, variant = _tile_window(i, n_tiles, radius)
        q_rows = pl.ds(pl.multiple_of(i * tq, tq), tq)
        k_rows = pl.ds(pl.multiple_of(k_start, radius), window)
        u, _, l = _tile_attention(q_ref[q_rows, :], k_ref[k_rows, :], v_ref[k_rows, :], bias_ref[variant], sink)
        o_ref[q_rows, :] = u / l
        return carry

    lax.fori_loop(0, n_tiles, tile, 0, unroll=BAND_UNROLL)


def _sink_window_attention(q, k, v, sink):
    batch, seq_len, width = q.shape
    assert seq_len % BAND_TILE == 0 and seq_len >= BAND_TILE + 2 * C_RADIUS
    bias = _window_bias(C_RADIUS)
    seq = pl.BlockSpec((None, seq_len, PAIR), lambda b, p: (b, 0, p))
    return pl.pallas_call(
        _sink_window_kernel,
        grid=(batch, width // PAIR),
        in_specs=[pl.BlockSpec(memory_space=pltpu.SMEM), seq, seq, seq, _resident(bias.shape)],
        out_specs=seq,
        out_shape=jax.ShapeDtypeStruct(q.shape, F32),
        compiler_params=_params(2),
    )(sink, q, k, v, bias)


def _dense_kernel(q_ref, k_ref, v_ref, o_ref, v1_ref):
    seq_len = k_ref.shape[0]

    @pl.when(pl.program_id(2) == 0)
    def _():
        v1_ref[:, :PAIR] = v_ref[...]
        v1_ref[:, PAIR:] = jnp.ones((seq_len, PAIR), BF16)

    q2 = _stack_heads(q_ref[...])
    m = acc = None
    for start in range(0, seq_len, DENSE_KEYS):
        keys = slice(start, start + DENSE_KEYS)
        s = lax.dot_general(q2, k_ref[keys, :], (((1,), (1,)), ((), ())), preferred_element_type=F32)
        m_chunk = jnp.max(s, axis=1, keepdims=True)
        m_new = m_chunk if m is None else jnp.maximum(m, m_chunk)
        pv = jnp.dot(jnp.exp2(s - m_new).astype(BF16), v1_ref[keys, :], preferred_element_type=F32)
        acc = pv if m is None else jnp.exp2(m - m_new) * acc + pv
        m = m_new
    o_ref[...] = _unstack_heads(acc[:, :PAIR] / acc[:, PAIR:])


def _dense_attention(q, k, v):
    batch, seq_len, width = q.shape
    n_pairs = width // PAIR
    q_spec = pl.BlockSpec((None, DENSE_TILE, PAIR), lambda b, p, i: (b, i, p))
    kv_spec = pl.BlockSpec((None, seq_len, PAIR), lambda b, p, i: (b, 0, p))
    return pl.pallas_call(
        _dense_kernel,
        grid=(batch, n_pairs, seq_len // DENSE_TILE),
        in_specs=[q_spec, kv_spec, kv_spec],
        out_specs=q_spec,
        out_shape=jax.ShapeDtypeStruct(q.shape, F32),
        scratch_shapes=[pltpu.VMEM((seq_len, 2 * PAIR), BF16)],
        compiler_params=_params(3),
    )(q, k, v)


_IN_REGIONS = {"aq": (A_HEADS, False, False), "ak": (A_HEADS, False, False), "av": (A_HEADS, False, False),
               "bq": (B_HEADS, False, True), "bk": (B_KV_HEADS, True, True), "bv": (B_KV_HEADS, True, True),
               "cq": (C_HEADS, False, False), "ck": (C_KV_HEADS, True, False), "cv": (C_KV_HEADS, True, False)}
_OUT_REGIONS = {"oa": _IN_REGIONS["aq"], "ob": _IN_REGIONS["bq"], "oc": _IN_REGIONS["cq"]}


def _to_pair_layout(t, heads, dup, axial):
    lead = t.shape[:-1]
    if axial:
        t = t.reshape(lead + (heads, 2, 2, HALF // 2))
        t = jnp.swapaxes(t, -3, -2)
    t = t.reshape(lead + (heads, 2, HALF))
    if dup:
        t = jnp.broadcast_to(t[..., :, :, None, :], lead + (heads, 2, 2, HALF))
    else:
        t = jnp.swapaxes(t.reshape(lead + (heads // 2, 2, 2, HALF)), -3, -2)
    return t.reshape(lead + (-1,))


def _pair_regions(t, axis, regions, order):
    t = jnp.moveaxis(t, axis, -1)
    parts, start = {}, 0
    for name, (heads, dup, axial) in regions.items():
        parts[name] = _to_pair_layout(t[..., start:start + heads * HEAD_DIM], heads, dup, axial)
        start += heads * HEAD_DIM
    return jnp.moveaxis(jnp.concatenate([parts[name] for name in order], axis=-1), -1, axis)


def kernel(x, ffn1_norm, ffn1_w_gate, ffn1_w_up, ffn1_w_down, mix_norm, w_in, a_q_norm, a_k_norm, b_q_norm, b_k_norm, c_q_norm, c_k_norm, c_sink, group_norm, w_out, ffn2_norm, ffn2_w_gate, ffn2_w_up, ffn2_w_down):
    batch, seq_len, d = x.shape
    assert seq_len % ROW_TILE == 0 and seq_len % PROJ_TILE == 0 and seq_len % DENSE_TILE == 0 and seq_len % DENSE_KEYS == 0
    depth = w_in.shape[0]
    tables = _rope_lane_tables(seq_len)
    head_mean = _head_mean_matrix()
    d1, dx = _lane_dims(False), _lane_dims(True)
    q_scale = LOG2_E / math.sqrt(HEAD_DIM)
    shape3 = lambda t: t.reshape(batch, seq_len, t.shape[-1])
    flat = lambda t: t.reshape(batch * seq_len, t.shape[-1])

    bf = lambda t: t.astype(BF16)
    row = lambda t: t.reshape(depth, 1, d)
    ffn1 = (row(ffn1_norm), bf(ffn1_w_gate), bf(ffn1_w_up), bf(ffn1_w_down))
    ffn2 = (row(ffn2_norm), bf(ffn2_w_gate), bf(ffn2_w_up), bf(ffn2_w_down))
    w_in_pairs = _pair_regions(bf(w_in), 2, _IN_REGIONS, _SEG_ORDER)
    w_out_pairs = _pair_regions(bf(w_out), 1, _OUT_REGIONS, tuple(_OUT_REGIONS))
    group_gain = row(_pair_regions(group_norm, 1, _OUT_REGIONS, tuple(_OUT_REGIONS)))
    gains = jnp.stack([a_q_norm[:, d1] * q_scale, a_k_norm[:, d1], b_q_norm[:, dx] * q_scale, b_k_norm[:, dx],
                       c_q_norm[:, d1] * q_scale, c_k_norm[:, d1]], axis=1).astype(F32)
    sinks = c_sink.astype(F32) * LOG2_E

    xf = x.reshape(batch * seq_len, d)
    for l in range(depth):
        xf = _half_swiglu(xf, l, *ffn1)
        seg = dict(zip(_SEG_ORDER, _project(xf, l, row(mix_norm), w_in_pairs, head_mean, gains, tables,
                                            batch, seq_len)))
        oa = _dilated_attention(seg["aq"], seg["ak"], seg["av"])
        ob = _dense_attention(shape3(seg["bq"]), shape3(seg["bk"]), shape3(seg["bv"]))
        oc = _sink_window_attention(shape3(seg["cq"]), shape3(seg["ck"]), shape3(seg["cv"]), sinks[l])
        xf = _half_swiglu(xf, l, *ffn2, mixers=(flat(oa), flat(ob), flat(oc), group_gain, w_out_pairs))
    return xf.reshape(batch, seq_len, d)
```

```python
import math

import numpy as np
import jax
import jax.numpy as jnp
from jax import lax
from jax.experimental import pallas as pl
from jax.experimental.pallas import tpu as pltpu

HEAD_DIM = 64
HALF = HEAD_DIM // 2
PAIR = 2 * HEAD_DIM
A_HEADS, B_HEADS, B_KV_HEADS, C_HEADS, C_KV_HEADS = 8, 4, 2, 4, 2
DILATIONS = (1, 4, 16)
QUAD = 4
A_RADIUS = 64
C_RADIUS = 128
GRID_W = 64
ROPE_THETA = 10000.0
EPS = 1e-6
MASK_VALUE = -1e30
LOG2_E = math.log2(math.e)
VMEM_LIMIT_BYTES = 56 * 1024 * 1024

ROW_TILE = 512
PROJ_TILE = 1024
PROJ_CHAINS = 4
BAND_TILE = 128
BAND_UNROLL = 16
DENSE_TILE = 1024
DENSE_KEYS = 256

F32 = jnp.float32
BF16 = jnp.bfloat16


def _lane_dims(axial):
    lane = np.arange(PAIR)
    half, idx = lane // HEAD_DIM, lane % HALF
    if not axial:
        return half * HALF + idx
    quarter = HALF // 2
    return np.where(idx < quarter, half * quarter + idx, HALF + half * quarter + (idx - quarter))


def _lane_slot():
    lane = np.arange(PAIR)
    return (lane % HEAD_DIM) // HALF


def _rope_dim_tables(pos, dim):
    inv_freq = 1.0 / (ROPE_THETA ** (jnp.arange(0, dim, 2, dtype=F32) / dim))
    ang = pos.astype(F32)[:, None] * inv_freq[None, :]
    ang = jnp.concatenate([ang, ang], axis=-1)
    sign = jnp.concatenate([-jnp.ones((dim // 2,), F32), jnp.ones((dim // 2,), F32)])
    return jnp.cos(ang), jnp.sin(ang) * sign[None, :]


def _rope_lane_tables(seq_len):
    t = jnp.arange(seq_len)
    cos1, sin1 = _rope_dim_tables(t, HEAD_DIM)
    cos_r, sin_r = _rope_dim_tables(t // GRID_W, HALF)
    cos_c, sin_c = _rope_dim_tables(t % GRID_W, HALF)
    cos_x = jnp.concatenate([cos_r, cos_c], axis=-1)
    sin_x = jnp.concatenate([sin_r, sin_c], axis=-1)
    d1, dx = _lane_dims(False), _lane_dims(True)
    return cos1[:, d1], sin1[:, d1], cos_x[:, dx], sin_x[:, dx]


def _head_mean_matrix():
    slot = np.concatenate([_lane_slot(), 2 + _lane_slot()])
    return jnp.asarray((slot[:, None] == slot[None, :]) / HEAD_DIM, dtype=BF16)


def _window_bias(radius, interleave=1):
    window = BAND_TILE + 2 * radius

    def offsets(count):
        index = np.arange(count)
        return (index % (count // interleave)) * interleave + index // (count // interleave)

    row = np.tile(offsets(BAND_TILE), 2)[:, None]
    col = offsets(window)[None, :]
    bias = [np.where(np.abs(col - off - row) <= radius, 0.0, MASK_VALUE) for off in (0, radius, 2 * radius)]
    return jnp.asarray(np.stack(bias), dtype=F32)


def _rms(x):
    return x * lax.rsqrt(jnp.mean(x * x, axis=-1, keepdims=True) + EPS)


def _params(n_axes):
    return pltpu.CompilerParams(dimension_semantics=("arbitrary",) * n_axes,
                                vmem_limit_bytes=VMEM_LIMIT_BYTES)


def _resident(shape, layer=None):
    if layer is None:
        return pl.BlockSpec(shape, lambda *_: (0,) * len(shape), pipeline_mode=pl.Buffered(1))
    return pl.BlockSpec((None,) + tuple(shape), lambda *_: (layer,) + (0,) * len(shape),
                        pipeline_mode=pl.Buffered(1))


def _swiglu_residual(x, g, wg_ref, wu_ref, wd_ref):
    h = (_rms(x) * g).astype(BF16)
    gate = jnp.dot(h, wg_ref[...], preferred_element_type=F32)
    up = jnp.dot(h, wu_ref[...], preferred_element_type=F32)
    act = (gate * (1.0 / (1.0 + jnp.exp(-gate))) * up).astype(BF16)
    return x + 0.5 * jnp.dot(act, wd_ref[...], preferred_element_type=F32)


def _ffn_kernel(x_ref, g_ref, wg_ref, wu_ref, wd_ref, o_ref):
    o_ref[...] = _swiglu_residual(x_ref[...], g_ref[...], wg_ref, wu_ref, wd_ref)


def _mix_ffn_kernel(x_ref, oa_ref, ob_ref, oc_ref, gg_ref, wo_ref, g_ref, wg_ref, wu_ref, wd_ref, o_ref):
    gg = gg_ref[...]
    wa, wb = oa_ref.shape[1], ob_ref.shape[1]
    mixed = jnp.concatenate([_rms(oa_ref[...]) * gg[:, :wa],
                             _rms(ob_ref[...]) * gg[:, wa:wa + wb],
                             _rms(oc_ref[...]) * gg[:, wa + wb:]], axis=1)
    x = x_ref[...] + jnp.dot(mixed.astype(BF16), wo_ref[...], preferred_element_type=F32)
    o_ref[...] = _swiglu_residual(x, g_ref[...], wg_ref, wu_ref, wd_ref)


def _half_swiglu(x, layer, g, wg, wu, wd, mixers=None):
    n, d = x.shape
    f = wg.shape[-1]
    rows = lambda width: pl.BlockSpec((ROW_TILE, width), lambda i: (i, 0))
    ffn_specs = [_resident((1, d), layer), _resident((d, f), layer), _resident((d, f), layer),
                 _resident((f, d), layer)]
    if mixers is None:
        body, in_specs, args = _ffn_kernel, [rows(d)] + ffn_specs, (x, g, wg, wu, wd)
    else:
        oa, ob, oc, gg, wo = mixers
        body = _mix_ffn_kernel
        in_specs = [rows(d), rows(oa.shape[1]), rows(ob.shape[1]), rows(oc.shape[1]),
                    _resident((1, d), layer), _resident((d, d), layer)] + ffn_specs
        args = (x, oa, ob, oc, gg, wo, g, wg, wu, wd)
    return pl.pallas_call(
        body,
        grid=(n // ROW_TILE,),
        in_specs=in_specs,
        out_specs=rows(d),
        out_shape=jax.ShapeDtypeStruct((n, d), F32),
        compiler_params=_params(1),
    )(*args)


_SEG = {"av": (0, 512), "aq": (512, 512), "ak": (1024, 512), "bv": (1536, 256), "cv": (1792, 256),
        "bq": (2048, 256), "bk": (2304, 256), "cq": (2560, 256), "ck": (2816, 256)}
_SEG_ORDER = tuple(_SEG)
PROJ_COLS = 3072


def _norm_rope(seg, head_mean, gain, cos, sin):
    mean_sq = jnp.dot((seg * seg).astype(BF16), head_mean, preferred_element_type=F32)
    y = seg * lax.rsqrt(mean_sq + EPS)
    out = []
    for k in range(2):
        yk = y[:, k * PAIR:(k + 1) * PAIR] * gain
        out.append(yk * cos + pltpu.roll(yk, HEAD_DIM, 1) * sin)
    return jnp.concatenate(out, axis=1)


def _proj_kernel(x_ref, g_ref, w_ref, hs_ref, gain_ref, cos1_ref, sin1_ref, cosx_ref, sinx_ref,
                 *refs):
    outs, stage_ref = dict(zip(_SEG_ORDER, refs[:-1])), refs[-1]
    head_mean = hs_ref[...]
    gain_row = {"aq": 0, "ak": 1, "bq": 2, "bk": 3, "cq": 4, "ck": 5}
    rows = x_ref.shape[0] // PROJ_CHAINS
    for chain in range(PROJ_CHAINS):
        span = slice(chain * rows, (chain + 1) * rows)
        h = (_rms(x_ref[span, :]) * g_ref[...]).astype(BF16)
        proj = jnp.dot(h, w_ref[...], preferred_element_type=F32)
        tables = {"a": (cos1_ref[span, :], sin1_ref[span, :]), "b": (cosx_ref[span, :], sinx_ref[span, :]),
                  "c": (cos1_ref[span, :], sin1_ref[span, :])}
        for name in _SEG_ORDER:
            start, width = _SEG[name]
            out = outs[name]
            for c in range(width // 256):
                seg = proj[:, start + c * 256:start + (c + 1) * 256]
                if name in gain_row:
                    gain = gain_ref[gain_row[name]:gain_row[name] + 1, :]
                    seg = _norm_rope(seg, head_mean, gain, *tables[name[0]])
                if name[0] != "a":
                    out[span, c * 256:(c + 1) * 256] = seg.astype(out.dtype)
                    continue
                quarter = slice(chain * (rows // QUAD), (chain + 1) * (rows // QUAD))
                for k in range(2):
                    slab = 2 * c + k
                    stage_ref[chain, slab] = seg[:, k * PAIR:(k + 1) * PAIR]
                    for r in range(QUAD):
                        out[r, quarter, slab * PAIR:(slab + 1) * PAIR] = (
                            stage_ref[chain, slab, pl.ds(r, rows // QUAD, stride=QUAD), :])


def _project(x, layer, g, w, head_mean, gains, tables, batch, seq_len):
    n, d = x.shape
    blocks_per_seq = seq_len // PROJ_TILE
    row = pl.BlockSpec((PROJ_TILE, d), lambda i: (i, 0))
    table = pl.BlockSpec((PROJ_TILE, PAIR), lambda i: (i % blocks_per_seq, 0))
    out_specs, out_shape = [], []
    for name in _SEG_ORDER:
        width = _SEG[name][1]
        if name[0] == "a":
            out_specs.append(pl.BlockSpec((None, QUAD, PROJ_TILE // QUAD, width),
                                          lambda i: (i // blocks_per_seq, 0, i % blocks_per_seq, 0)))
            out_shape.append(jax.ShapeDtypeStruct((batch, QUAD, seq_len // QUAD, width), F32))
        else:
            out_specs.append(pl.BlockSpec((PROJ_TILE, width), lambda i: (i, 0)))
            out_shape.append(jax.ShapeDtypeStruct((n, width), BF16))
    return pl.pallas_call(
        _proj_kernel,
        grid=(n // PROJ_TILE,),
        in_specs=[row, _resident((1, d), layer), _resident((d, PROJ_COLS), layer), _resident((256, 256)),
                  _resident(gains.shape[1:], layer), table, table, table, table],
        out_specs=out_specs,
        out_shape=out_shape,
        scratch_shapes=[pltpu.VMEM((PROJ_CHAINS, _SEG["aq"][1] // PAIR, PROJ_TILE // PROJ_CHAINS, PAIR), F32)],
        compiler_params=_params(1),
    )(x, g, w, head_mean, gains, *tables)


def _own_lanes(shape):
    lane = lax.broadcasted_iota(jnp.int32, shape, 1)
    return (lane & HALF) == 0


def _stack_heads(q):
    own = _own_lanes(q.shape)
    zero = jnp.zeros_like(q)
    return jnp.concatenate([jnp.where(own, q, zero), jnp.where(own, zero, q)], axis=0)


def _unstack_heads(x2):
    rows = x2.shape[0] // 2
    return jnp.where(_own_lanes((rows, PAIR)), x2[:rows], x2[rows:])


def _tile_attention(q, k, v, bias, sink=None):
    tq, window = q.shape[0], k.shape[0]
    s = lax.dot_general(_stack_heads(q.astype(BF16)), k.astype(BF16), (((1,), (1,)), ((), ())),
                        preferred_element_type=F32) + bias
    m2 = jnp.max(s, axis=1, keepdims=True)
    if sink is not None:
        m2 = jnp.maximum(m2, sink)
    p = jnp.exp2(s - m2).astype(BF16)
    v1 = jnp.concatenate([v.astype(BF16), jnp.ones((window, PAIR), BF16)], axis=1)
    ul = jnp.dot(p, v1, preferred_element_type=F32)
    l2 = ul[:, PAIR:]
    if sink is not None:
        l2 = l2 + jnp.exp2(sink - m2)
    return (_unstack_heads(ul[:, :PAIR]), _unstack_heads(jnp.broadcast_to(m2, (2 * tq, PAIR))),
            _unstack_heads(l2))


def _merge(u, m, l, u_old, m_old, l_old):
    m_new = jnp.maximum(m_old, m)
    w_old, w_new = jnp.exp2(m_old - m_new), jnp.exp2(m - m_new)
    return w_old * u_old + w_new * u, m_new, w_old * l_old + w_new * l


def _tile_window(i, n_tiles, radius):
    window = BAND_TILE + 2 * radius
    k_start = jnp.clip(i * BAND_TILE - radius, 0, n_tiles * BAND_TILE - window)
    return k_start, jnp.where(i == 0, 0, jnp.where(i == n_tiles - 1, 2, 1))


def _dilated_kernel(q_ref, k_ref, v_ref, bias_ref, bias1_ref, o_ref, u_acc, m_acc, l_acc):
    quarter = q_ref.shape[1]
    tq, radius = BAND_TILE, A_RADIUS
    window = tq + 2 * radius
    accs = (u_acc, m_acc, l_acc)

    n4, n16, n1 = quarter // tq, quarter // QUAD // tq, QUAD * quarter // tq

    def tile4(r, i):
        k_start, variant = _tile_window(i, n4, radius)
        q_rows = pl.ds(pl.multiple_of(i * tq, tq), tq)
        k_rows = pl.ds(pl.multiple_of(k_start, radius), window)
        part = _tile_attention(q_ref[r, q_rows, :], k_ref[r, k_rows, :], v_ref[r, k_rows, :], bias_ref[variant])
        for acc, val in zip(accs, part):
            acc[r, q_rows, :] = val

    def tile16(a, r, i):
        k_start, variant = _tile_window(i, n16, radius)
        q_rows = pl.ds(i * tq * QUAD + a, tq, stride=QUAD)
        k_rows = pl.ds(k_start * QUAD + a, window, stride=QUAD)
        part = _tile_attention(q_ref[r, q_rows, :], k_ref[r, k_rows, :], v_ref[r, k_rows, :], bias_ref[variant])
        part = _merge(*part, *(acc[r, q_rows, :] for acc in accs))
        for acc, val in zip(accs, part):
            acc[r, q_rows, :] = val

    def tile1(i):
        k_start, variant = _tile_window(i, n1, radius)
        q_rows = pl.ds(pl.multiple_of(i * (tq // QUAD), tq // QUAD), tq // QUAD)
        k_rows = pl.ds(pl.multiple_of(k_start // QUAD, radius // QUAD), window // QUAD)
        gather = lambda ref, rows: jnp.concatenate([ref[r, rows, :] for r in range(QUAD)], axis=0)
        part = _tile_attention(gather(q_ref, q_rows), gather(k_ref, k_rows), gather(v_ref, k_rows),
                               bias1_ref[variant])
        u, _, l = _merge(*part, *(gather(acc, q_rows) for acc in accs))
        out = u / l
        for r in range(QUAD):
            o_ref[pl.ds(i * tq + r, tq // QUAD, stride=QUAD), :] = out[r * (tq // QUAD):(r + 1) * (tq // QUAD)]

    def span(s, carry):
        for r in range(QUAD):
            for j in range(QUAD):
                tile4(r, s * QUAD + j)
        for a in range(QUAD):
            for r in range(QUAD):
                tile16(a, r, s)
        for j in range(QUAD * QUAD):
            tile1(s * QUAD * QUAD + j)
        return carry

    lax.fori_loop(0, n16, span, 0)


def _dilated_attention(q, k, v):
    batch, _, quarter, width = q.shape
    seq_len = QUAD * quarter
    window = BAND_TILE + 2 * A_RADIUS
    assert DILATIONS == (1, QUAD, QUAD * QUAD) and quarter % (QUAD * BAND_TILE) == 0
    assert seq_len // (QUAD * QUAD) >= window
    bias = _window_bias(A_RADIUS)
    bias1 = _window_bias(A_RADIUS, interleave=QUAD)
    quad = pl.BlockSpec((None, QUAD, quarter, PAIR), lambda b, p: (b, 0, 0, p))
    return pl.pallas_call(
        _dilated_kernel,
        grid=(batch, width // PAIR),
        in_specs=[quad, quad, quad, _resident(bias.shape), _resident(bias1.shape)],
        out_specs=pl.BlockSpec((None, seq_len, PAIR), lambda b, p: (b, 0, p)),
        out_shape=jax.ShapeDtypeStruct((batch, seq_len, width), F32),
        scratch_shapes=[pltpu.VMEM((QUAD, quarter, PAIR), F32)] * 3,
        compiler_params=_params(2),
    )(q, k, v, bias, bias1)


def _sink_window_kernel(sink_ref, q_ref, k_ref, v_ref, bias_ref, o_ref):
    tq, radius = BAND_TILE, C_RADIUS
    window = tq + 2 * radius
    n_tiles = q_ref.shape[0] // tq
    pair = pl.program_id(1)
    first_head = lax.broadcasted_iota(jnp.int32, (2 * tq, 1), 0) < tq
    sink = jnp.where(first_head, sink_ref[2 * pair], sink_ref[2 * pair + 1])

    def tile(i, carry):
        k_start, variant = _tile_window(i, n_tiles, radius)
        q_rows = pl.ds(pl.multiple_of(i * tq, tq), tq)
        k_rows = pl.ds(pl.multiple_of(k_start, radius), window)
        u, _, l = _tile_attention(q_ref[q_rows, :], k_ref[k_rows, :], v_ref[k_rows, :], bias_ref[variant], sink)
        o_ref[q_rows, :] = u / l
        return carry

    lax.fori_loop(0, n_tiles, tile, 0, unroll=BAND_UNROLL)


def _sink_window_attention(q, k, v, sink):
    batch, seq_len, width = q.shape
    assert seq_len % BAND_TILE == 0 and seq_len >= BAND_TILE + 2 * C_RADIUS
    bias = _window_bias(C_RADIUS)
    seq = pl.BlockSpec((None, seq_len, PAIR), lambda b, p: (b, 0, p))
    return pl.pallas_call(
        _sink_window_kernel,
        grid=(batch, width // PAIR),
        in_specs=[pl.BlockSpec(memory_space=pltpu.SMEM), seq, seq, seq, _resident(bias.shape)],
        out_specs=seq,
        out_shape=jax.ShapeDtypeStruct(q.shape, F32),
        compiler_params=_params(2),
    )(sink, q, k, v, bias)


def _dense_kernel(q_ref, k_ref, v_ref, o_ref, v1_ref):
    seq_len = k_ref.shape[0]

    @pl.when(pl.program_id(2) == 0)
    def _():
        v1_ref[:, :PAIR] = v_ref[...]
        v1_ref[:, PAIR:] = jnp.ones((seq_len, PAIR), BF16)

    q2 = _stack_heads(q_ref[...])
    m = acc = None
    for start in range(0, seq_len, DENSE_KEYS):
        keys = slice(start, start + DENSE_KEYS)
        s = lax.dot_general(q2, k_ref[keys, :], (((1,), (1,)), ((), ())), preferred_element_type=F32)
        m_chunk = jnp.max(s, axis=1, keepdims=True)
        m_new = m_chunk if m is None else jnp.maximum(m, m_chunk)
        pv = jnp.dot(jnp.exp2(s - m_new).astype(BF16), v1_ref[keys, :], preferred_element_type=F32)
        acc = pv if m is None else jnp.exp2(m - m_new) * acc + pv
        m = m_new
    o_ref[...] = _unstack_heads(acc[:, :PAIR] / acc[:, PAIR:])


def _dense_attention(q, k, v):
    batch, seq_len, width = q.shape
    n_pairs = width // PAIR
    q_spec = pl.BlockSpec((None, DENSE_TILE, PAIR), lambda b, p, i: (b, i, p))
    kv_spec = pl.BlockSpec((None, seq_len, PAIR), lambda b, p, i: (b, 0, p))
    return pl.pallas_call(
        _dense_kernel,
        grid=(batch, n_pairs, seq_len // DENSE_TILE),
        in_specs=[q_spec, kv_spec, kv_spec],
        out_specs=q_spec,
        out_shape=jax.ShapeDtypeStruct(q.shape, F32),
        scratch_shapes=[pltpu.VMEM((seq_len, 2 * PAIR), BF16)],
        compiler_params=_params(3),
    )(q, k, v)


_IN_REGIONS = {"aq": (A_HEADS, False, False), "ak": (A_HEADS, False, False), "av": (A_HEADS, False, False),
               "bq": (B_HEADS, False, True), "bk": (B_KV_HEADS, True, True), "bv": (B_KV_HEADS, True, True),
               "cq": (C_HEADS, False, False), "ck": (C_KV_HEADS, True, False), "cv": (C_KV_HEADS, True, False)}
_OUT_REGIONS = {"oa": _IN_REGIONS["aq"], "ob": _IN_REGIONS["bq"], "oc": _IN_REGIONS["cq"]}


def _to_pair_layout(t, heads, dup, axial):
    lead = t.shape[:-1]
    if axial:
        t = t.reshape(lead + (heads, 2, 2, HALF // 2))
        t = jnp.swapaxes(t, -3, -2)
    t = t.reshape(lead + (heads, 2, HALF))
    if dup:
        t = jnp.broadcast_to(t[..., :, :, None, :], lead + (heads, 2, 2, HALF))
    else:
        t = jnp.swapaxes(t.reshape(lead + (heads // 2, 2, 2, HALF)), -3, -2)
    return t.reshape(lead + (-1,))


def _pair_regions(t, axis, regions, order):
    t = jnp.moveaxis(t, axis, -1)
    parts, start = {}, 0
    for name, (heads, dup, axial) in regions.items():
        parts[name] = _to_pair_layout(t[..., start:start + heads * HEAD_DIM], heads, dup, axial)
        start += heads * HEAD_DIM
    return jnp.moveaxis(jnp.concatenate([parts[name] for name in order], axis=-1), -1, axis)


def kernel(x, ffn1_norm, ffn1_w_gate, ffn1_w_up, ffn1_w_down, mix_norm, w_in, a_q_norm, a_k_norm, b_q_norm, b_k_norm, c_q_norm, c_k_norm, c_sink, group_norm, w_out, ffn2_norm, ffn2_w_gate, ffn2_w_up, ffn2_w_down):
    batch, seq_len, d = x.shape
    assert seq_len % ROW_TILE == 0 and seq_len % PROJ_TILE == 0 and seq_len % DENSE_TILE == 0 and seq_len % DENSE_KEYS == 0
    depth = w_in.shape[0]
    tables = _rope_lane_tables(seq_len)
    head_mean = _head_mean_matrix()
    d1, dx = _lane_dims(False), _lane_dims(True)
    q_scale = LOG2_E / math.sqrt(HEAD_DIM)
    shape3 = lambda t: t.reshape(batch, seq_len, t.shape[-1])
    flat = lambda t: t.reshape(batch * seq_len, t.shape[-1])

    bf = lambda t: t.astype(BF16)
    row = lambda t: t.reshape(depth, 1, d)
    ffn1 = (row(ffn1_norm), bf(ffn1_w_gate), bf(ffn1_w_up), bf(ffn1_w_down))
    ffn2 = (row(ffn2_norm), bf(ffn2_w_gate), bf(ffn2_w_up), bf(ffn2_w_down))
    w_in_pairs = _pair_regions(bf(w_in), 2, _IN_REGIONS, _SEG_ORDER)
    w_out_pairs = _pair_regions(bf(w_out), 1, _OUT_REGIONS, tuple(_OUT_REGIONS))
    group_gain = row(_pair_regions(group_norm, 1, _OUT_REGIONS, tuple(_OUT_REGIONS)))
    gains = jnp.stack([a_q_norm[:, d1] * q_scale, a_k_norm[:, d1], b_q_norm[:, dx] * q_scale, b_k_norm[:, dx],
                       c_q_norm[:, d1] * q_scale, c_k_norm[:, d1]], axis=1).astype(F32)
    sinks = c_sink.astype(F32) * LOG2_E

    xf = x.reshape(batch * seq_len, d)
    for l in range(depth):
        xf = _half_swiglu(xf, l, *ffn1)
        seg = dict(zip(_SEG_ORDER, _project(xf, l, row(mix_norm), w_in_pairs, head_mean, gains, tables,
                                            batch, seq_len)))
        oa = _dilated_attention(seg["aq"], seg["ak"], seg["av"])
        ob = _dense_attention(shape3(seg["bq"]), shape3(seg["bk"]), shape3(seg["bv"]))
        oc = _sink_window_attention(shape3(seg["cq"]), shape3(seg["ck"]), shape3(seg["cv"]), sinks[l])
        xf = _half_swiglu(xf, l, *ffn2, mixers=(flat(oa), flat(ob), flat(oc), group_gain, w_out_pairs))
    return xf.reshape(batch, seq_len, d)
```

```python
import math

import numpy as np
import jax
import jax.numpy as jnp
from jax import lax
from jax.experimental import pallas as pl
from jax.experimental.pallas import tpu as pltpu

HEAD_DIM = 64
HALF = HEAD_DIM // 2
PAIR = 2 * HEAD_DIM
A_HEADS, B_HEADS, B_KV_HEADS, C_HEADS, C_KV_HEADS = 8, 4, 2, 4, 2
DILATIONS = (1, 4, 16)
QUAD = 4
A_RADIUS = 64
C_RADIUS = 128
GRID_W = 64
ROPE_THETA = 10000.0
EPS = 1e-6
MASK_VALUE = -1e30
LOG2_E = math.log2(math.e)
VMEM_LIMIT_BYTES = 56 * 1024 * 1024

ROW_TILE = 512
PROJ_TILE = 1024
PROJ_CHAINS = 8
BAND_TILE = 128
BAND_UNROLL = 16
DENSE_TILE = 2048
DENSE_KEYS = 256

F32 = jnp.float32
BF16 = jnp.bfloat16


def _lane_dims(axial):
    lane = np.arange(PAIR)
    half, idx = lane // HEAD_DIM, lane % HALF
    if not axial:
        return half * HALF + idx
    quarter = HALF // 2
    return np.where(idx < quarter, half * quarter + idx, HALF + half * quarter + (idx - quarter))


def _lane_slot():
    lane = np.arange(PAIR)
    return (lane % HEAD_DIM) // HALF


def _rope_dim_tables(pos, dim):
    inv_freq = 1.0 / (ROPE_THETA ** (jnp.arange(0, dim, 2, dtype=F32) / dim))
    ang = pos.astype(F32)[:, None] * inv_freq[None, :]
    ang = jnp.concatenate([ang, ang], axis=-1)
    sign = jnp.concatenate([-jnp.ones((dim // 2,), F32), jnp.ones((dim // 2,), F32)])
    return jnp.cos(ang), jnp.sin(ang) * sign[None, :]


def _rope_lane_tables(seq_len):
    t = jnp.arange(seq_len)
    cos1, sin1 = _rope_dim_tables(t, HEAD_DIM)
    cos_r, sin_r = _rope_dim_tables(t // GRID_W, HALF)
    cos_c, sin_c = _rope_dim_tables(t % GRID_W, HALF)
    cos_x = jnp.concatenate([cos_r, cos_c], axis=-1)
    sin_x = jnp.concatenate([sin_r, sin_c], axis=-1)
    d1, dx = _lane_dims(False), _lane_dims(True)
    return cos1[:, d1], sin1[:, d1], cos_x[:, dx], sin_x[:, dx]


def _head_mean_matrix():
    slot = np.concatenate([_lane_slot(), 2 + _lane_slot()])
    return jnp.asarray((slot[:, None] == slot[None, :]) / HEAD_DIM, dtype=BF16)


def _window_bias(radius, interleave=1):
    window = BAND_TILE + 2 * radius

    def offsets(count):
        index = np.arange(count)
        return (index % (count // interleave)) * interleave + index // (count // interleave)

    row = np.tile(offsets(BAND_TILE), 2)[:, None]
    col = offsets(window)[None, :]
    bias = [np.where(np.abs(col - off - row) <= radius, 0.0, MASK_VALUE) for off in (0, radius, 2 * radius)]
    return jnp.asarray(np.stack(bias), dtype=F32)


def _rms(x):
    return x * lax.rsqrt(jnp.mean(x * x, axis=-1, keepdims=True) + EPS)


def _params(n_axes):
    return pltpu.CompilerParams(dimension_semantics=("arbitrary",) * n_axes,
                                vmem_limit_bytes=VMEM_LIMIT_BYTES)


def _resident(shape, layer=None):
    if layer is None:
        return pl.BlockSpec(shape, lambda *_: (0,) * len(shape), pipeline_mode=pl.Buffered(1))
    return pl.BlockSpec((None,) + tuple(shape), lambda *_: (layer,) + (0,) * len(shape),
                        pipeline_mode=pl.Buffered(1))


def _swiglu_residual(x, g, wg_ref, wu_ref, wd_ref):
    h = (_rms(x) * g).astype(BF16)
    gate = jnp.dot(h, wg_ref[...], preferred_element_type=F32)
    up = jnp.dot(h, wu_ref[...], preferred_element_type=F32)
    act = (gate * (1.0 / (1.0 + jnp.exp(-gate))) * up).astype(BF16)
    return x + 0.5 * jnp.dot(act, wd_ref[...], preferred_element_type=F32)


def _ffn_kernel(x_ref, g_ref, wg_ref, wu_ref, wd_ref, o_ref):
    o_ref[...] = _swiglu_residual(x_ref[...], g_ref[...], wg_ref, wu_ref, wd_ref)


def _mix_ffn_kernel(x_ref, oa_ref, ob_ref, oc_ref, gg_ref, wo_ref, g_ref, wg_ref, wu_ref, wd_ref, o_ref):
    gg = gg_ref[...]
    wa, wb = oa_ref.shape[1], ob_ref.shape[1]
    mixed = jnp.concatenate([_rms(oa_ref[...]) * gg[:, :wa],
                             _rms(ob_ref[...]) * gg[:, wa:wa + wb],
                             _rms(oc_ref[...]) * gg[:, wa + wb:]], axis=1)
    x = x_ref[...] + jnp.dot(mixed.astype(BF16), wo_ref[...], preferred_element_type=F32)
    o_ref[...] = _swiglu_residual(x, g_ref[...], wg_ref, wu_ref, wd_ref)


def _half_swiglu(x, layer, g, wg, wu, wd, mixers=None):
    n, d = x.shape
    f = wg.shape[-1]
    rows = lambda width: pl.BlockSpec((ROW_TILE, width), lambda i: (i, 0))
    ffn_specs = [_resident((1, d), layer), _resident((d, f), layer), _resident((d, f), layer),
                 _resident((f, d), layer)]
    if mixers is None:
        body, in_specs, args = _ffn_kernel, [rows(d)] + ffn_specs, (x, g, wg, wu, wd)
    else:
        oa, ob, oc, gg, wo = mixers
        body = _mix_ffn_kernel
        in_specs = [rows(d), rows(oa.shape[1]), rows(ob.shape[1]), rows(oc.shape[1]),
                    _resident((1, d), layer), _resident((d, d), layer)] + ffn_specs
        args = (x, oa, ob, oc, gg, wo, g, wg, wu, wd)
    return pl.pallas_call(
        body,
        grid=(n // ROW_TILE,),
        in_specs=in_specs,
        out_specs=rows(d),
        out_shape=jax.ShapeDtypeStruct((n, d), F32),
        compiler_params=_params(1),
    )(*args)


_SEG = {"av": (0, 512), "aq": (512, 512), "ak": (1024, 512), "bv": (1536, 256), "cv": (1792, 256),
        "bq": (2048, 256), "bk": (2304, 256), "cq": (2560, 256), "ck": (2816, 256)}
_SEG_ORDER = tuple(_SEG)
PROJ_COLS = 3072


def _norm_rope(seg, head_mean, gain, cos, sin):
    mean_sq = jnp.dot((seg * seg).astype(BF16), head_mean, preferred_element_type=F32)
    y = seg * lax.rsqrt(mean_sq + EPS)
    out = []
    for k in range(2):
        yk = y[:, k * PAIR:(k + 1) * PAIR] * gain
        out.append(yk * cos + pltpu.roll(yk, HEAD_DIM, 1) * sin)
    return jnp.concatenate(out, axis=1)


def _proj_kernel(x_ref, g_ref, w_ref, hs_ref, gain_ref, cos1_ref, sin1_ref, cosx_ref, sinx_ref,
                 *refs):
    outs, stage_ref = dict(zip(_SEG_ORDER, refs[:-1])), refs[-1]
    head_mean = hs_ref[...]
    gain_row = {"aq": 0, "ak": 1, "bq": 2, "bk": 3, "cq": 4, "ck": 5}
    rows = x_ref.shape[0] // PROJ_CHAINS
    for chain in range(PROJ_CHAINS):
        span = slice(chain * rows, (chain + 1) * rows)
        h = (_rms(x_ref[span, :]) * g_ref[...]).astype(BF16)
        proj = jnp.dot(h, w_ref[...], preferred_element_type=F32)
        tables = {"a": (cos1_ref[span, :], sin1_ref[span, :]), "b": (cosx_ref[span, :], sinx_ref[span, :]),
                  "c": (cos1_ref[span, :], sin1_ref[span, :])}
        for name in _SEG_ORDER:
            start, width = _SEG[name]
            out = outs[name]
            for c in range(width // 256):
                seg = proj[:, start + c * 256:start + (c + 1) * 256]
                if name in gain_row:
                    gain = gain_ref[gain_row[name]:gain_row[name] + 1, :]
                    seg = _norm_rope(seg, head_mean, gain, *tables[name[0]])
                if name[0] != "a":
                    out[span, c * 256:(c + 1) * 256] = seg.astype(out.dtype)
                    continue
                quarter = slice(chain * (rows // QUAD), (chain + 1) * (rows // QUAD))
                for k in range(2):
                    slab = 2 * c + k
                    stage_ref[chain, slab] = seg[:, k * PAIR:(k + 1) * PAIR]
                    for r in range(QUAD):
                        out[r, quarter, slab * PAIR:(slab + 1) * PAIR] = (
                            stage_ref[chain, slab, pl.ds(r, rows // QUAD, stride=QUAD), :])


def _project(x, layer, g, w, head_mean, gains, tables, batch, seq_len):
    n, d = x.shape
    blocks_per_seq = seq_len // PROJ_TILE
    row = pl.BlockSpec((PROJ_TILE, d), lambda i: (i, 0))
    table = pl.BlockSpec((PROJ_TILE, PAIR), lambda i: (i % blocks_per_seq, 0))
    out_specs, out_shape = [], []
    for name in _SEG_ORDER:
        width = _SEG[name][1]
        if name[0] == "a":
            out_specs.append(pl.BlockSpec((None, QUAD, PROJ_TILE // QUAD, width),
                                          lambda i: (i // blocks_per_seq, 0, i % blocks_per_seq, 0)))
            out_shape.append(jax.ShapeDtypeStruct((batch, QUAD, seq_len // QUAD, width), F32))
        else:
            out_specs.append(pl.BlockSpec((PROJ_TILE, width), lambda i: (i, 0)))
            out_shape.append(jax.ShapeDtypeStruct((n, width), BF16))
    return pl.pallas_call(
        _proj_kernel,
        grid=(n // PROJ_TILE,),
        in_specs=[row, _resident((1, d), layer), _resident((d, PROJ_COLS), layer), _resident((256, 256)),
                  _resident(gains.shape[1:], layer), table, table, table, table],
        out_specs=out_specs,
        out_shape=out_shape,
        scratch_shapes=[pltpu.VMEM((PROJ_CHAINS, _SEG["aq"][1] // PAIR, PROJ_TILE // PROJ_CHAINS, PAIR), F32)],
        compiler_params=_params(1),
    )(x, g, w, head_mean, gains, *tables)


def _own_lanes(shape):
    lane = lax.broadcasted_iota(jnp.int32, shape, 1)
    return (lane & HALF) == 0


def _stack_heads(q):
    own = _own_lanes(q.shape)
    zero = jnp.zeros_like(q)
    return jnp.concatenate([jnp.where(own, q, zero), jnp.where(own, zero, q)], axis=0)


def _unstack_heads(x2):
    rows = x2.shape[0] // 2
    return jnp.where(_own_lanes((rows, PAIR)), x2[:rows], x2[rows:])


def _tile_attention(q, k, v, bias, sink=None):
    tq, window = q.shape[0], k.shape[0]
    s = lax.dot_general(_stack_heads(q.astype(BF16)), k.astype(BF16), (((1,), (1,)), ((), ())),
                        preferred_element_type=F32) + bias
    m2 = jnp.max(s, axis=1, keepdims=True)
    if sink is not None:
        m2 = jnp.maximum(m2, sink)
    p = jnp.exp2(s - m2).astype(BF16)
    v1 = jnp.concatenate([v.astype(BF16), jnp.ones((window, PAIR), BF16)], axis=1)
    ul = jnp.dot(p, v1, preferred_element_type=F32)
    l2 = ul[:, PAIR:]
    if sink is not None:
        l2 = l2 + jnp.exp2(sink - m2)
    return (_unstack_heads(ul[:, :PAIR]), _unstack_heads(jnp.broadcast_to(m2, (2 * tq, PAIR))),
            _unstack_heads(l2))


def _merge(u, m, l, u_old, m_old, l_old):
    m_new = jnp.maximum(m_old, m)
    w_old, w_new = jnp.exp2(m_old - m_new), jnp.exp2(m - m_new)
    return w_old * u_old + w_new * u, m_new, w_old * l_old + w_new * l


def _tile_window(i, n_tiles, radius):
    window = BAND_TILE + 2 * radius
    k_start = jnp.clip(i * BAND_TILE - radius, 0, n_tiles * BAND_TILE - window)
    return k_start, jnp.where(i == 0, 0, jnp.where(i == n_tiles - 1, 2, 1))


def _dilated_kernel(q_ref, k_ref, v_ref, bias_ref, bias1_ref, o_ref, u_acc, m_acc, l_acc):
    quarter = q_ref.shape[1]
    tq, radius = BAND_TILE, A_RADIUS
    window = tq + 2 * radius
    accs = (u_acc, m_acc, l_acc)

    n4, n16, n1 = quarter // tq, quarter // QUAD // tq, QUAD * quarter // tq

    def tile4(r, i):
        k_start, variant = _tile_window(i, n4, radius)
        q_rows = pl.ds(pl.multiple_of(i * tq, tq), tq)
        k_rows = pl.ds(pl.multiple_of(k_start, radius), window)
        part = _tile_attention(q_ref[r, q_rows, :], k_ref[r, k_rows, :], v_ref[r, k_rows, :], bias_ref[variant])
        for acc, val in zip(accs, part):
            acc[r, q_rows, :] = val

    def tile16(a, r, i):
        k_start, variant = _tile_window(i, n16, radius)
        q_rows = pl.ds(i * tq * QUAD + a, tq, stride=QUAD)
        k_rows = pl.ds(k_start * QUAD + a, window, stride=QUAD)
        part = _tile_attention(q_ref[r, q_rows, :], k_ref[r, k_rows, :], v_ref[r, k_rows, :], bias_ref[variant])
        part = _merge(*part, *(acc[r, q_rows, :] for acc in accs))
        for acc, val in zip(accs, part):
            acc[r, q_rows, :] = val

    def tile1(i):
        k_start, variant = _tile_window(i, n1, radius)
        q_rows = pl.ds(pl.multiple_of(i * (tq // QUAD), tq // QUAD), tq // QUAD)
        k_rows = pl.ds(pl.multiple_of(k_start // QUAD, radius // QUAD), window // QUAD)
        gather = lambda ref, rows: jnp.concatenate([ref[r, rows, :] for r in range(QUAD)], axis=0)
        part = _tile_attention(gather(q_ref, q_rows), gather(k_ref, k_rows), gather(v_ref, k_rows),
                               bias1_ref[variant])
        u, _, l = _merge(*part, *(gather(acc, q_rows) for acc in accs))
        out = u / l
        for r in range(QUAD):
            o_ref[pl.ds(i * tq + r, tq // QUAD, stride=QUAD), :] = out[r * (tq // QUAD):(r + 1) * (tq // QUAD)]

    def span(s, carry):
        for r in range(QUAD):
            for j in range(QUAD):
                tile4(r, s * QUAD + j)
        for a in range(QUAD):
            for r in range(QUAD):
                tile16(a, r, s)
        for j in range(QUAD * QUAD):
            tile1(s * QUAD * QUAD + j)
        return carry

    lax.fori_loop(0, n16, span, 0)


def _dilated_attention(q, k, v):
    batch, _, quarter, width = q.shape
    seq_len = QUAD * quarter
    window = BAND_TILE + 2 * A_RADIUS
    assert DILATIONS == (1, QUAD, QUAD * QUAD) and quarter % (QUAD * BAND_TILE) == 0
    assert seq_len // (QUAD * QUAD) >= window
    bias = _window_bias(A_RADIUS)
    bias1 = _window_bias(A_RADIUS, interleave=QUAD)
    quad = pl.BlockSpec((None, QUAD, quarter, PAIR), lambda b, p: (b, 0, 0, p))
    return pl.pallas_call(
        _dilated_kernel,
        grid=(batch, width // PAIR),
        in_specs=[quad, quad, quad, _resident(bias.shape), _resident(bias1.shape)],
        out_specs=pl.BlockSpec((None, seq_len, PAIR), lambda b, p: (b, 0, p)),
        out_shape=jax.ShapeDtypeStruct((batch, seq_len, width), F32),
        scratch_shapes=[pltpu.VMEM((QUAD, quarter, PAIR), F32)] * 3,
        compiler_params=_params(2),
    )(q, k, v, bias, bias1)


def _sink_window_kernel(sink_ref, q_ref, k_ref, v_ref, bias_ref, o_ref):
    tq, radius = BAND_TILE, C_RADIUS
    window = tq + 2 * radius
    n_tiles = q_ref.shape[0] // tq
    pair = pl.program_id(1)
    first_head = lax.broadcasted_iota(jnp.int32, (2 * tq, 1), 0) < tq
    sink = jnp.where(first_head, sink_ref[2 * pair], sink_ref[2 * pair + 1])

    def tile(i, carry):
        k_start, variant = _tile_window(i, n_tiles, radius)
        q_rows = pl.ds(pl.multiple_of(i * tq, tq), tq)
        k_rows = pl.ds(pl.multiple_of(k_start, radius), window)
        u, _, l = _tile_attention(q_ref[q_rows, :], k_ref[k_rows, :], v_ref[k_rows, :], bias_ref[variant], sink)
        o_ref[q_rows, :] = u / l
        return carry

    lax.fori_loop(0, n_tiles, tile, 0, unroll=BAND_UNROLL)


def _sink_window_attention(q, k, v, sink):
    batch, seq_len, width = q.shape
    assert seq_len % BAND_TILE == 0 and seq_len >= BAND_TILE + 2 * C_RADIUS
    bias = _window_bias(C_RADIUS)
    seq = pl.BlockSpec((None, seq_len, PAIR), lambda b, p: (b, 0, p))
    return pl.pallas_call(
        _sink_window_kernel,
        grid=(batch, width // PAIR),
        in_specs=[pl.BlockSpec(memory_space=pltpu.SMEM), seq, seq, seq, _resident(bias.shape)],
        out_specs=seq,
        out_shape=jax.ShapeDtypeStruct(q.shape, F32),
        compiler_params=_params(2),
    )(sink, q, k, v, bias)


def _dense_kernel(q_ref, k_ref, v_ref, o_ref, v1_ref):
    seq_len = k_ref.shape[0]

    @pl.when(pl.program_id(2) == 0)
    def _():
        v1_ref[:, :PAIR] = v_ref[...]
        v1_ref[:, PAIR:] = jnp.ones((seq_len, PAIR), BF16)

    q2 = _stack_heads(q_ref[...])
    m = acc = None
    for start in range(0, seq_len, DENSE_KEYS):
        keys = slice(start, start + DENSE_KEYS)
        s = lax.dot_general(q2, k_ref[keys, :], (((1,), (1,)), ((), ())), preferred_element_type=F32)
        m_chunk = jnp.max(s, axis=1, keepdims=True)
        m_new = m_chunk if m is None else jnp.maximum(m, m_chunk)
        pv = jnp.dot(jnp.exp2(s - m_new).astype(BF16), v1_ref[keys, :], preferred_element_type=F32)
        acc = pv if m is None else jnp.exp2(m - m_new) * acc + pv
        m = m_new
    o_ref[...] = _unstack_heads(acc[:, :PAIR] / acc[:, PAIR:])


def _dense_attention(q, k, v):
    batch, seq_len, width = q.shape
    n_pairs = width // PAIR
    q_spec = pl.BlockSpec((None, DENSE_TILE, PAIR), lambda b, p, i: (b, i, p))
    kv_spec = pl.BlockSpec((None, seq_len, PAIR), lambda b, p, i: (b, 0, p))
    return pl.pallas_call(
        _dense_kernel,
        grid=(batch, n_pairs, seq_len // DENSE_TILE),
        in_specs=[q_spec, kv_spec, kv_spec],
        out_specs=q_spec,
        out_shape=jax.ShapeDtypeStruct(q.shape, F32),
        scratch_shapes=[pltpu.VMEM((seq_len, 2 * PAIR), BF16)],
        compiler_params=_params(3),
    )(q, k, v)


_IN_REGIONS = {"aq": (A_HEADS, False, False), "ak": (A_HEADS, False, False), "av": (A_HEADS, False, False),
               "bq": (B_HEADS, False, True), "bk": (B_KV_HEADS, True, True), "bv": (B_KV_HEADS, True, True),
               "cq": (C_HEADS, False, False), "ck": (C_KV_HEADS, True, False), "cv": (C_KV_HEADS, True, False)}
_OUT_REGIONS = {"oa": _IN_REGIONS["aq"], "ob": _IN_REGIONS["bq"], "oc": _IN_REGIONS["cq"]}


def _to_pair_layout(t, heads, dup, axial):
    lead = t.shape[:-1]
    if axial:
        t = t.reshape(lead + (heads, 2, 2, HALF // 2))
        t = jnp.swapaxes(t, -3, -2)
    t = t.reshape(lead + (heads, 2, HALF))
    if dup:
        t = jnp.broadcast_to(t[..., :, :, None, :], lead + (heads, 2, 2, HALF))
    else:
        t = jnp.swapaxes(t.reshape(lead + (heads // 2, 2, 2, HALF)), -3, -2)
    return t.reshape(lead + (-1,))


def _pair_regions(t, axis, regions, order):
    t = jnp.moveaxis(t, axis, -1)
    parts, start = {}, 0
    for name, (heads, dup, axial) in regions.items():
        parts[name] = _to_pair_layout(t[..., start:start + heads * HEAD_DIM], heads, dup, axial)
        start += heads * HEAD_DIM
    return jnp.moveaxis(jnp.concatenate([parts[name] for name in order], axis=-1), -1, axis)


def kernel(x, ffn1_norm, ffn1_w_gate, ffn1_w_up, ffn1_w_down, mix_norm, w_in, a_q_norm, a_k_norm, b_q_norm, b_k_norm, c_q_norm, c_k_norm, c_sink, group_norm, w_out, ffn2_norm, ffn2_w_gate, ffn2_w_up, ffn2_w_down):
    batch, seq_len, d = x.shape
    assert seq_len % ROW_TILE == 0 and seq_len % PROJ_TILE == 0 and seq_len % DENSE_TILE == 0 and seq_len % DENSE_KEYS == 0
    depth = w_in.shape[0]
    tables = _rope_lane_tables(seq_len)
    head_mean = _head_mean_matrix()
    d1, dx = _lane_dims(False), _lane_dims(True)
    q_scale = LOG2_E / math.sqrt(HEAD_DIM)
    shape3 = lambda t: t.reshape(batch, seq_len, t.shape[-1])
    flat = lambda t: t.reshape(batch * seq_len, t.shape[-1])

    bf = lambda t: t.astype(BF16)
    row = lambda t: t.reshape(depth, 1, d)
    ffn1 = (row(ffn1_norm), bf(ffn1_w_gate), bf(ffn1_w_up), bf(ffn1_w_down))
    ffn2 = (row(ffn2_norm), bf(ffn2_w_gate), bf(ffn2_w_up), bf(ffn2_w_down))
    w_in_pairs = _pair_regions(bf(w_in), 2, _IN_REGIONS, _SEG_ORDER)
    w_out_pairs = _pair_regions(bf(w_out), 1, _OUT_REGIONS, tuple(_OUT_REGIONS))
    group_gain = row(_pair_regions(group_norm, 1, _OUT_REGIONS, tuple(_OUT_REGIONS)))
    gains = jnp.stack([a_q_norm[:, d1] * q_scale, a_k_norm[:, d1], b_q_norm[:, dx] * q_scale, b_k_norm[:, dx],
                       c_q_norm[:, d1] * q_scale, c_k_norm[:, d1]], axis=1).astype(F32)
    sinks = c_sink.astype(F32) * LOG2_E

    xf = x.reshape(batch * seq_len, d)
    for l in range(depth):
        xf = _half_swiglu(xf, l, *ffn1)
        seg = dict(zip(_SEG_ORDER, _project(xf, l, row(mix_norm), w_in_pairs, head_mean, gains, tables,
                                            batch, seq_len)))
        oa = _dilated_attention(seg["aq"], seg["ak"], seg["av"])
        ob = _dense_attention(shape3(seg["bq"]), shape3(seg["bk"]), shape3(seg["bv"]))
        oc = _sink_window_attention(shape3(seg["cq"]), shape3(seg["ck"]), shape3(seg["cv"]), sinks[l])
        xf = _half_swiglu(xf, l, *ffn2, mixers=(flat(oa), flat(ob), flat(oc), group_gain, w_out_pairs))
    return xf.reshape(batch, seq_len, d)
```

```python
import math

import numpy as np
import jax
import jax.numpy as jnp
from jax import lax
from jax.experimental import pallas as pl
from jax.experimental.pallas import tpu as pltpu

HEAD_DIM = 64
HALF = HEAD_DIM // 2
PAIR = 2 * HEAD_DIM
A_HEADS, B_HEADS, B_KV_HEADS, C_HEADS, C_KV_HEADS = 8, 4, 2, 4, 2
DILATIONS = (1, 4, 16)
QUAD = 4
A_RADIUS = 64
C_RADIUS = 128
GRID_W = 64
ROPE_THETA = 10000.0
EPS = 1e-6
MASK_VALUE = -1e30
LOG2_E = math.log2(math.e)
VMEM_LIMIT_BYTES = 56 * 1024 * 1024

ROW_TILE = 512
PROJ_TILE = 1024
PROJ_CHAINS = 8
BAND_TILE = 128
SINK_TILE = 256
BAND_UNROLL = 16
DENSE_TILE = 2048
DENSE_KEYS = 256

F32 = jnp.float32
BF16 = jnp.bfloat16


def _lane_dims(axial):
    lane = np.arange(PAIR)
    half, idx = lane // HEAD_DIM, lane % HALF
    if not axial:
        return half * HALF + idx
    quarter = HALF // 2
    return np.where(idx < quarter, half * quarter + idx, HALF + half * quarter + (idx - quarter))


def _lane_slot():
    lane = np.arange(PAIR)
    return (lane % HEAD_DIM) // HALF


def _rope_dim_tables(pos, dim):
    inv_freq = 1.0 / (ROPE_THETA ** (jnp.arange(0, dim, 2, dtype=F32) / dim))
    ang = pos.astype(F32)[:, None] * inv_freq[None, :]
    ang = jnp.concatenate([ang, ang], axis=-1)
    sign = jnp.concatenate([-jnp.ones((dim // 2,), F32), jnp.ones((dim // 2,), F32)])
    return jnp.cos(ang), jnp.sin(ang) * sign[None, :]


def _rope_lane_tables(seq_len):
    t = jnp.arange(seq_len)
    cos1, sin1 = _rope_dim_tables(t, HEAD_DIM)
    cos_r, sin_r = _rope_dim_tables(t // GRID_W, HALF)
    cos_c, sin_c = _rope_dim_tables(t % GRID_W, HALF)
    cos_x = jnp.concatenate([cos_r, cos_c], axis=-1)
    sin_x = jnp.concatenate([sin_r, sin_c], axis=-1)
    d1, dx = _lane_dims(False), _lane_dims(True)
    return cos1[:, d1], sin1[:, d1], cos_x[:, dx], sin_x[:, dx]


def _head_mean_matrix():
    slot = np.concatenate([_lane_slot(), 2 + _lane_slot()])
    return jnp.asarray((slot[:, None] == slot[None, :]) / HEAD_DIM, dtype=BF16)


def _window_bias(radius, interleave=1, tile=BAND_TILE):
    window = tile + 2 * radius

    def offsets(count):
        index = np.arange(count)
        return (index % (count // interleave)) * interleave + index // (count // interleave)

    row = np.tile(offsets(tile), 2)[:, None]
    col = offsets(window)[None, :]
    bias = [np.where(np.abs(col - off - row) <= radius, 0.0, MASK_VALUE) for off in (0, radius, 2 * radius)]
    return jnp.asarray(np.stack(bias), dtype=F32)


def _rms(x):
    return x * lax.rsqrt(jnp.mean(x * x, axis=-1, keepdims=True) + EPS)


def _params(n_axes):
    return pltpu.CompilerParams(dimension_semantics=("arbitrary",) * n_axes,
                                vmem_limit_bytes=VMEM_LIMIT_BYTES)


def _resident(shape, layer=None):
    if layer is None:
        return pl.BlockSpec(shape, lambda *_: (0,) * len(shape), pipeline_mode=pl.Buffered(1))
    return pl.BlockSpec((None,) + tuple(shape), lambda *_: (layer,) + (0,) * len(shape),
                        pipeline_mode=pl.Buffered(1))


def _swiglu_residual(x, g, wg_ref, wu_ref, wd_ref):
    h = (_rms(x) * g).astype(BF16)
    gate = jnp.dot(h, wg_ref[...], preferred_element_type=F32)
    up = jnp.dot(h, wu_ref[...], preferred_element_type=F32)
    act = (gate * (1.0 / (1.0 + jnp.exp(-gate))) * up).astype(BF16)
    return x + 0.5 * jnp.dot(act, wd_ref[...], preferred_element_type=F32)


def _ffn_kernel(x_ref, g_ref, wg_ref, wu_ref, wd_ref, o_ref):
    o_ref[...] = _swiglu_residual(x_ref[...], g_ref[...], wg_ref, wu_ref, wd_ref)


def _mix_ffn_kernel(x_ref, oa_ref, ob_ref, oc_ref, gg_ref, wo_ref, g_ref, wg_ref, wu_ref, wd_ref, o_ref):
    gg = gg_ref[...]
    wa, wb = oa_ref.shape[1], ob_ref.shape[1]
    mixed = jnp.concatenate([_rms(oa_ref[...]) * gg[:, :wa],
                             _rms(ob_ref[...]) * gg[:, wa:wa + wb],
                             _rms(oc_ref[...]) * gg[:, wa + wb:]], axis=1)
    x = x_ref[...] + jnp.dot(mixed.astype(BF16), wo_ref[...], preferred_element_type=F32)
    o_ref[...] = _swiglu_residual(x, g_ref[...], wg_ref, wu_ref, wd_ref)


def _half_swiglu(x, layer, g, wg, wu, wd, mixers=None):
    n, d = x.shape
    f = wg.shape[-1]
    rows = lambda width: pl.BlockSpec((ROW_TILE, width), lambda i: (i, 0))
    ffn_specs = [_resident((1, d), layer), _resident((d, f), layer), _resident((d, f), layer),
                 _resident((f, d), layer)]
    if mixers is None:
        body, in_specs, args = _ffn_kernel, [rows(d)] + ffn_specs, (x, g, wg, wu, wd)
    else:
        oa, ob, oc, gg, wo = mixers
        body = _mix_ffn_kernel
        in_specs = [rows(d), rows(oa.shape[1]), rows(ob.shape[1]), rows(oc.shape[1]),
                    _resident((1, d), layer), _resident((d, d), layer)] + ffn_specs
        args = (x, oa, ob, oc, gg, wo, g, wg, wu, wd)
    return pl.pallas_call(
        body,
        grid=(n // ROW_TILE,),
        in_specs=in_specs,
        out_specs=rows(d),
        out_shape=jax.ShapeDtypeStruct((n, d), F32),
        compiler_params=_params(1),
    )(*args)


_SEG = {"av": (0, 512), "aq": (512, 512), "ak": (1024, 512), "bv": (1536, 256), "cv": (1792, 256),
        "bq": (2048, 256), "bk": (2304, 256), "cq": (2560, 256), "ck": (2816, 256)}
_SEG_ORDER = tuple(_SEG)
PROJ_COLS = 3072


def _norm_rope(seg, head_mean, gain, cos, sin):
    mean_sq = jnp.dot((seg * seg).astype(BF16), head_mean, preferred_element_type=F32)
    y = seg * lax.rsqrt(mean_sq + EPS)
    out = []
    for k in range(2):
        yk = y[:, k * PAIR:(k + 1) * PAIR] * gain
        out.append(yk * cos + pltpu.roll(yk, HEAD_DIM, 1) * sin)
    return jnp.concatenate(out, axis=1)


def _proj_kernel(x_ref, g_ref, w_ref, hs_ref, gain_ref, cos1_ref, sin1_ref, cosx_ref, sinx_ref,
                 *refs):
    outs, stage_ref = dict(zip(_SEG_ORDER, refs[:-1])), refs[-1]
    head_mean = hs_ref[...]
    gain_row = {"aq": 0, "ak": 1, "bq": 2, "bk": 3, "cq": 4, "ck": 5}
    rows = x_ref.shape[0] // PROJ_CHAINS
    for chain in range(PROJ_CHAINS):
        span = slice(chain * rows, (chain + 1) * rows)
        h = (_rms(x_ref[span, :]) * g_ref[...]).astype(BF16)
        proj = jnp.dot(h, w_ref[...], preferred_element_type=F32)
        tables = {"a": (cos1_ref[span, :], sin1_ref[span, :]), "b": (cosx_ref[span, :], sinx_ref[span, :]),
                  "c": (cos1_ref[span, :], sin1_ref[span, :])}
        for name in _SEG_ORDER:
            start, width = _SEG[name]
            out = outs[name]
            for c in range(width // 256):
                seg = proj[:, start + c * 256:start + (c + 1) * 256]
                if name in gain_row:
                    gain = gain_ref[gain_row[name]:gain_row[name] + 1, :]
                    seg = _norm_rope(seg, head_mean, gain, *tables[name[0]])
                if name[0] != "a":
                    out[span, c * 256:(c + 1) * 256] = seg.astype(out.dtype)
                    continue
                quarter = slice(chain * (rows // QUAD), (chain + 1) * (rows // QUAD))
                for k in range(2):
                    slab = 2 * c + k
                    stage_ref[chain, slab] = seg[:, k * PAIR:(k + 1) * PAIR]
                    for r in range(QUAD):
                        out[r, quarter, slab * PAIR:(slab + 1) * PAIR] = (
                            stage_ref[chain, slab, pl.ds(r, rows // QUAD, stride=QUAD), :])


def _project(x, layer, g, w, head_mean, gains, tables, batch, seq_len):
    n, d = x.shape
    blocks_per_seq = seq_len // PROJ_TILE
    row = pl.BlockSpec((PROJ_TILE, d), lambda i: (i, 0))
    table = pl.BlockSpec((PROJ_TILE, PAIR), lambda i: (i % blocks_per_seq, 0))
    out_specs, out_shape = [], []
    for name in _SEG_ORDER:
        width = _SEG[name][1]
        if name[0] == "a":
            out_specs.append(pl.BlockSpec((None, QUAD, PROJ_TILE // QUAD, width),
                                          lambda i: (i // blocks_per_seq, 0, i % blocks_per_seq, 0)))
            out_shape.append(jax.ShapeDtypeStruct((batch, QUAD, seq_len // QUAD, width), F32))
        else:
            out_specs.append(pl.BlockSpec((PROJ_TILE, width), lambda i: (i, 0)))
            out_shape.append(jax.ShapeDtypeStruct((n, width), BF16))
    return pl.pallas_call(
        _proj_kernel,
        grid=(n // PROJ_TILE,),
        in_specs=[row, _resident((1, d), layer), _resident((d, PROJ_COLS), layer), _resident((256, 256)),
                  _resident(gains.shape[1:], layer), table, table, table, table],
        out_specs=out_specs,
        out_shape=out_shape,
        scratch_shapes=[pltpu.VMEM((PROJ_CHAINS, _SEG["aq"][1] // PAIR, PROJ_TILE // PROJ_CHAINS, PAIR), F32)],
        compiler_params=_params(1),
    )(x, g, w, head_mean, gains, *tables)


def _own_lanes(shape):
    lane = lax.broadcasted_iota(jnp.int32, shape, 1)
    return (lane & HALF) == 0


def _stack_heads(q):
    own = _own_lanes(q.shape)
    zero = jnp.zeros_like(q)
    return jnp.concatenate([jnp.where(own, q, zero), jnp.where(own, zero, q)], axis=0)


def _unstack_heads(x2):
    rows = x2.shape[0] // 2
    return jnp.where(_own_lanes((rows, PAIR)), x2[:rows], x2[rows:])


def _tile_attention(q, k, v, bias, sink=None):
    tq, window = q.shape[0], k.shape[0]
    s = lax.dot_general(_stack_heads(q.astype(BF16)), k.astype(BF16), (((1,), (1,)), ((), ())),
                        preferred_element_type=F32) + bias
    m2 = jnp.max(s, axis=1, keepdims=True)
    if sink is not None:
        m2 = jnp.maximum(m2, sink)
    p = jnp.exp2(s - m2).astype(BF16)
    v1 = jnp.concatenate([v.astype(BF16), jnp.ones((window, PAIR), BF16)], axis=1)
    ul = jnp.dot(p, v1, preferred_element_type=F32)
    l2 = ul[:, PAIR:]
    if sink is not None:
        l2 = l2 + jnp.exp2(sink - m2)
    return (_unstack_heads(ul[:, :PAIR]), _unstack_heads(jnp.broadcast_to(m2, (2 * tq, PAIR))),
            _unstack_heads(l2))


def _merge(u, m, l, u_old, m_old, l_old):
    m_new = jnp.maximum(m_old, m)
    w_old, w_new = jnp.exp2(m_old - m_new), jnp.exp2(m - m_new)
    return w_old * u_old + w_new * u, m_new, w_old * l_old + w_new * l


def _tile_window(i, n_tiles, radius, tile=BAND_TILE):
    window = tile + 2 * radius
    k_start = jnp.clip(i * tile - radius, 0, n_tiles * tile - window)
    return k_start, jnp.where(i == 0, 0, jnp.where(i == n_tiles - 1, 2, 1))


def _dilated_kernel(q_ref, k_ref, v_ref, bias_ref, bias1_ref, o_ref, u_acc, m_acc, l_acc):
    quarter = q_ref.shape[1]
    tq, radius = BAND_TILE, A_RADIUS
    window = tq + 2 * radius
    accs = (u_acc, m_acc, l_acc)

    n4, n16, n1 = quarter // tq, quarter // QUAD // tq, QUAD * quarter // tq

    def tile4(r, i):
        k_start, variant = _tile_window(i, n4, radius)
        q_rows = pl.ds(pl.multiple_of(i * tq, tq), tq)
        k_rows = pl.ds(pl.multiple_of(k_start, radius), window)
        part = _tile_attention(q_ref[r, q_rows, :], k_ref[r, k_rows, :], v_ref[r, k_rows, :], bias_ref[variant])
        for acc, val in zip(accs, part):
            acc[r, q_rows, :] = val

    def tile16(a, r, i):
        k_start, variant = _tile_window(i, n16, radius)
        q_rows = pl.ds(i * tq * QUAD + a, tq, stride=QUAD)
        k_rows = pl.ds(k_start * QUAD + a, window, stride=QUAD)
        part = _tile_attention(q_ref[r, q_rows, :], k_ref[r, k_rows, :], v_ref[r, k_rows, :], bias_ref[variant])
        part = _merge(*part, *(acc[r, q_rows, :] for acc in accs))
        for acc, val in zip(accs, part):
            acc[r, q_rows, :] = val

    def tile1(i):
        k_start, variant = _tile_window(i, n1, radius)
        q_rows = pl.ds(pl.multiple_of(i * (tq // QUAD), tq // QUAD), tq // QUAD)
        k_rows = pl.ds(pl.multiple_of(k_start // QUAD, radius // QUAD), window // QUAD)
        gather = lambda ref, rows: jnp.concatenate([ref[r, rows, :] for r in range(QUAD)], axis=0)
        part = _tile_attention(gather(q_ref, q_rows), gather(k_ref, k_rows), gather(v_ref, k_rows),
                               bias1_ref[variant])
        u, _, l = _merge(*part, *(gather(acc, q_rows) for acc in accs))
        out = u / l
        for r in range(QUAD):
            o_ref[pl.ds(i * tq + r, tq // QUAD, stride=QUAD), :] = out[r * (tq // QUAD):(r + 1) * (tq // QUAD)]

    def span(s, carry):
        for r in range(QUAD):
            for j in range(QUAD):
                tile4(r, s * QUAD + j)
        for a in range(QUAD):
            for r in range(QUAD):
                tile16(a, r, s)
        for j in range(QUAD * QUAD):
            tile1(s * QUAD * QUAD + j)
        return carry

    lax.fori_loop(0, n16, span, 0)


def _dilated_attention(q, k, v):
    batch, _, quarter, width = q.shape
    seq_len = QUAD * quarter
    window = BAND_TILE + 2 * A_RADIUS
    assert DILATIONS == (1, QUAD, QUAD * QUAD) and quarter % (QUAD * BAND_TILE) == 0
    assert seq_len // (QUAD * QUAD) >= window
    bias = _window_bias(A_RADIUS)
    bias1 = _window_bias(A_RADIUS, interleave=QUAD)
    quad = pl.BlockSpec((None, QUAD, quarter, PAIR), lambda b, p: (b, 0, 0, p))
    return pl.pallas_call(
        _dilated_kernel,
        grid=(batch, width // PAIR),
        in_specs=[quad, quad, quad, _resident(bias.shape), _resident(bias1.shape)],
        out_specs=pl.BlockSpec((None, seq_len, PAIR), lambda b, p: (b, 0, p)),
        out_shape=jax.ShapeDtypeStruct((batch, seq_len, width), F32),
        scratch_shapes=[pltpu.VMEM((QUAD, quarter, PAIR), F32)] * 3,
        compiler_params=_params(2),
    )(q, k, v, bias, bias1)


def _sink_window_kernel(sink_ref, q_ref, k_ref, v_ref, bias_ref, o_ref):
    tq, radius = SINK_TILE, C_RADIUS
    window = tq + 2 * radius
    n_tiles = q_ref.shape[0] // tq
    pair = pl.program_id(1)
    first_head = lax.broadcasted_iota(jnp.int32, (2 * tq, 1), 0) < tq
    sink = jnp.where(first_head, sink_ref[2 * pair], sink_ref[2 * pair + 1])

    def tile(i, carry):
        k_start, variant = _tile_window(i, n_tiles, radius, tq)
        q_rows = pl.ds(pl.multiple_of(i * tq, tq), tq)
        k_rows = pl.ds(pl.multiple_of(k_start, radius), window)
        u, _, l = _tile_attention(q_ref[q_rows, :], k_ref[k_rows, :], v_ref[k_rows, :], bias_ref[variant], sink)
        o_ref[q_rows, :] = u / l
        return carry

    lax.fori_loop(0, n_tiles, tile, 0, unroll=BAND_UNROLL // 2)


def _sink_window_attention(q, k, v, sink):
    batch, seq_len, width = q.shape
    assert seq_len % SINK_TILE == 0 and seq_len >= SINK_TILE + 2 * C_RADIUS
    bias = _window_bias(C_RADIUS, tile=SINK_TILE)
    seq = pl.BlockSpec((None, seq_len, PAIR), lambda b, p: (b, 0, p))
    return pl.pallas_call(
        _sink_window_kernel,
        grid=(batch, width // PAIR),
        in_specs=[pl.BlockSpec(memory_space=pltpu.SMEM), seq, seq, seq, _resident(bias.shape)],
        out_specs=seq,
        out_shape=jax.ShapeDtypeStruct(q.shape, F32),
        compiler_params=_params(2),
    )(sink, q, k, v, bias)


def _dense_kernel(q_ref, k_ref, v_ref, o_ref, v1_ref):
    seq_len = k_ref.shape[0]

    @pl.when(pl.program_id(2) == 0)
    def _():
        v1_ref[:, :PAIR] = v_ref[...]
        v1_ref[:, PAIR:] = jnp.ones((seq_len, PAIR), BF16)

    q2 = _stack_heads(q_ref[...])
    m = acc = None
    for start in range(0, seq_len, DENSE_KEYS):
        keys = slice(start, start + DENSE_KEYS)
        s = lax.dot_general(q2, k_ref[keys, :], (((1,), (1,)), ((), ())), preferred_element_type=F32)
        m_chunk = jnp.max(s, axis=1, keepdims=True)
        m_new = m_chunk if m is None else jnp.maximum(m, m_chunk)
        pv = jnp.dot(jnp.exp2(s - m_new).astype(BF16), v1_ref[keys, :], preferred_element_type=F32)
        acc = pv if m is None else jnp.exp2(m - m_new) * acc + pv
        m = m_new
    o_ref[...] = _unstack_heads(acc[:, :PAIR] / acc[:, PAIR:])


def _dense_attention(q, k, v):
    batch, seq_len, width = q.shape
    n_pairs = width // PAIR
    q_spec = pl.BlockSpec((None, DENSE_TILE, PAIR), lambda b, p, i: (b, i, p))
    kv_spec = pl.BlockSpec((None, seq_len, PAIR), lambda b, p, i: (b, 0, p))
    return pl.pallas_call(
        _dense_kernel,
        grid=(batch, n_pairs, seq_len // DENSE_TILE),
        in_specs=[q_spec, kv_spec, kv_spec],
        out_specs=q_spec,
        out_shape=jax.ShapeDtypeStruct(q.shape, F32),
        scratch_shapes=[pltpu.VMEM((seq_len, 2 * PAIR), BF16)],
        compiler_params=_params(3),
    )(q, k, v)


_IN_REGIONS = {"aq": (A_HEADS, False, False), "ak": (A_HEADS, False, False), "av": (A_HEADS, False, False),
               "bq": (B_HEADS, False, True), "bk": (B_KV_HEADS, True, True), "bv": (B_KV_HEADS, True, True),
               "cq": (C_HEADS, False, False), "ck": (C_KV_HEADS, True, False), "cv": (C_KV_HEADS, True, False)}
_OUT_REGIONS = {"oa": _IN_REGIONS["aq"], "ob": _IN_REGIONS["bq"], "oc": _IN_REGIONS["cq"]}


def _to_pair_layout(t, heads, dup, axial):
    lead = t.shape[:-1]
    if axial:
        t = t.reshape(lead + (heads, 2, 2, HALF // 2))
        t = jnp.swapaxes(t, -3, -2)
    t = t.reshape(lead + (heads, 2, HALF))
    if dup:
        t = jnp.broadcast_to(t[..., :, :, None, :], lead + (heads, 2, 2, HALF))
    else:
        t = jnp.swapaxes(t.reshape(lead + (heads // 2, 2, 2, HALF)), -3, -2)
    return t.reshape(lead + (-1,))


def _pair_regions(t, axis, regions, order):
    t = jnp.moveaxis(t, axis, -1)
    parts, start = {}, 0
    for name, (heads, dup, axial) in regions.items():
        parts[name] = _to_pair_layout(t[..., start:start + heads * HEAD_DIM], heads, dup, axial)
        start += heads * HEAD_DIM
    return jnp.moveaxis(jnp.concatenate([parts[name] for name in order], axis=-1), -1, axis)


def kernel(x, ffn1_norm, ffn1_w_gate, ffn1_w_up, ffn1_w_down, mix_norm, w_in, a_q_norm, a_k_norm, b_q_norm, b_k_norm, c_q_norm, c_k_norm, c_sink, group_norm, w_out, ffn2_norm, ffn2_w_gate, ffn2_w_up, ffn2_w_down):
    batch, seq_len, d = x.shape
    assert seq_len % ROW_TILE == 0 and seq_len % PROJ_TILE == 0 and seq_len % DENSE_TILE == 0 and seq_len % DENSE_KEYS == 0
    depth = w_in.shape[0]
    tables = _rope_lane_tables(seq_len)
    head_mean = _head_mean_matrix()
    d1, dx = _lane_dims(False), _lane_dims(True)
    q_scale = LOG2_E / math.sqrt(HEAD_DIM)
    shape3 = lambda t: t.reshape(batch, seq_len, t.shape[-1])
    flat = lambda t: t.reshape(batch * seq_len, t.shape[-1])

    bf = lambda t: t.astype(BF16)
    row = lambda t: t.reshape(depth, 1, d)
    ffn1 = (row(ffn1_norm), bf(ffn1_w_gate), bf(ffn1_w_up), bf(ffn1_w_down))
    ffn2 = (row(ffn2_norm), bf(ffn2_w_gate), bf(ffn2_w_up), bf(ffn2_w_down))
    w_in_pairs = _pair_regions(bf(w_in), 2, _IN_REGIONS, _SEG_ORDER)
    w_out_pairs = _pair_regions(bf(w_out), 1, _OUT_REGIONS, tuple(_OUT_REGIONS))
    group_gain = row(_pair_regions(group_norm, 1, _OUT_REGIONS, tuple(_OUT_REGIONS)))
    gains = jnp.stack([a_q_norm[:, d1] * q_scale, a_k_norm[:, d1], b_q_norm[:, dx] * q_scale, b_k_norm[:, dx],
                       c_q_norm[:, d1] * q_scale, c_k_norm[:, d1]], axis=1).astype(F32)
    sinks = c_sink.astype(F32) * LOG2_E

    xf = x.reshape(batch * seq_len, d)
    for l in range(depth):
        xf = _half_swiglu(xf, l, *ffn1)
        seg = dict(zip(_SEG_ORDER, _project(xf, l, row(mix_norm), w_in_pairs, head_mean, gains, tables,
                                            batch, seq_len)))
        oa = _dilated_attention(seg["aq"], seg["ak"], seg["av"])
        ob = _dense_attention(shape3(seg["bq"]), shape3(seg["bk"]), shape3(seg["bv"]))
        oc = _sink_window_attention(shape3(seg["cq"]), shape3(seg["ck"]), shape3(seg["cv"]), sinks[l])
        xf = _half_swiglu(xf, l, *ffn2, mixers=(flat(oa), flat(ob), flat(oc), group_gain, w_out_pairs))
    return xf.reshape(batch, seq_len, d)
```
